```python
import math
import jax, jax.numpy as jnp
from jax import lax
import numpy as np

D_MODEL = 2048
BATCH = 16
SEQ = 2048
DEPTH = 4

CHUNK = 64
Q_BLOCK = 128
N_A_LAYERS = DEPTH // 2
N_B_LAYERS = DEPTH - N_A_LAYERS
HEAD_DIM = 128
A_HEADS = D_MODEL // (2 * HEAD_DIM)
A_QK_WIDTH = A_HEADS * 2 * HEAD_DIM
A_V_WIDTH = A_HEADS * 2 * HEAD_DIM
B_HEADS = D_MODEL // HEAD_DIM
B_WIDTH = B_HEADS * HEAD_DIM
ROT_DIM = HEAD_DIM // 4
ROPE_THETA = 500000.0
D_FF = 256 * ((8 * D_MODEL // 3 + 255) // 256)
CONV_WIDTH = 3
EPS = 1e-6

kernel_name = 'yoco_diffattn_stickbreaking_convffn'


def _rmsnorm(x, g):
    xf = x.astype(jnp.float32)
    y = xf * lax.rsqrt(jnp.mean(xf * xf, axis=-1, keepdims=True) + EPS)
    return (y * g.astype(jnp.float32)).astype(x.dtype)


def _rotary_tables(positions):
    inv_freq = ROPE_THETA ** (-jnp.arange(0, ROT_DIM, 2, dtype=jnp.float32) / ROT_DIM)
    ang = positions.astype(jnp.float32)[..., None] * inv_freq
    return jnp.cos(ang), jnp.sin(ang)


def _apply_partial_rotary(x, cos, sin):
    cos = cos[:, :, None, None, :]
    sin = sin[:, :, None, None, :]
    half = ROT_DIM // 2
    xr = x[..., :ROT_DIM].astype(jnp.float32)
    x1, x2 = xr[..., :half], xr[..., half:]
    rot = jnp.concatenate([x1 * cos - x2 * sin, x2 * cos + x1 * sin], axis=-1).astype(x.dtype)
    return jnp.concatenate([rot, x[..., ROT_DIM:]], axis=-1)


def _diff_attention(xn, w_qkv, q_g, k_g, lq1, lk1, lq2, lk2, subln_g, w_o, cos, sin, lambda_init):
    b, s, _ = xn.shape
    qkv = xn @ w_qkv
    q = qkv[..., :A_QK_WIDTH].reshape(b, s, A_HEADS, 2, HEAD_DIM)
    k = qkv[..., A_QK_WIDTH:2 * A_QK_WIDTH].reshape(b, s, A_HEADS, 2, HEAD_DIM)
    v = qkv[..., 2 * A_QK_WIDTH:].reshape(b, s, A_HEADS, 2 * HEAD_DIM)
    q = _apply_partial_rotary(_rmsnorm(q, q_g), cos, sin)
    k = _apply_partial_rotary(_rmsnorm(k, k_g), cos, sin)
    f32 = jnp.float32
    lam = (jnp.exp(jnp.sum(lq1.astype(f32) * lk1.astype(f32)))
           - jnp.exp(jnp.sum(lq2.astype(f32) * lk2.astype(f32))) + lambda_init)
    scale = HEAD_DIM ** -0.5
    outs = []
    for i0 in range(0, s, Q_BLOCK):
        kend = i0 + Q_BLOCK
        sc = jnp.einsum('bqhcd,bkhcd->bhcqk', q[:, i0:kend], k[:, :kend],
                        preferred_element_type=f32) * scale
        q_chunk = (i0 + jnp.arange(Q_BLOCK)) // CHUNK
        k_chunk = jnp.arange(kend) // CHUNK
        mask = k_chunk[None, :] <= q_chunk[:, None]
        p = jax.nn.softmax(jnp.where(mask, sc, -jnp.inf), axis=-1)
        w = p[:, :, 0] - lam * p[:, :, 1]
        outs.append(jnp.einsum('bhqk,bkhe->bqhe', w.astype(v.dtype), v[:, :kend]))
    o = jnp.concatenate(outs, axis=1)
    o = _rmsnorm(o, subln_g) * (1.0 - lambda_init)
    return o.reshape(b, s, A_V_WIDTH) @ w_o


def _stick_breaking(xn, w_q, k, v, w_o):
    b, s, _ = xn.shape
    q = (xn @ w_q).reshape(b, s, B_HEADS, HEAD_DIM)
    scale = HEAD_DIM ** -0.5
    outs = []
    for i0 in range(0, s, Q_BLOCK):
        kend = i0 + Q_BLOCK
        z = jnp.einsum('bqhd,bkhd->bhqk', q[:, i0:kend], k[:, :kend],
                       preferred_element_type=jnp.float32) * scale
        q_pos = i0 + jnp.arange(Q_BLOCK)
        k_pos = jnp.arange(kend)
        mask = k_pos[None, :] < q_pos[:, None]
        log_keep = jnp.where(mask, -jax.nn.softplus(z), 0.0)
        between = lax.cumsum(log_keep, axis=3, reverse=True) - log_keep
        log_a = jax.nn.log_sigmoid(z) + between
        a = jnp.where(mask, jnp.exp(log_a), 0.0)
        outs.append(jnp.einsum('bhqk,bkhd->bqhd', a.astype(v.dtype), v[:, :kend]))
    o = jnp.concatenate(outs, axis=1)
    return o.reshape(b, s, B_WIDTH) @ w_o


def _conv_ffn(xn, w_up, conv_w, conv_b, w_down):
    up = xn @ w_up
    u, g = up[..., :D_FF], up[..., D_FF:]
    g = lax.conv_general_dilated(g, conv_w[:, None, :], window_strides=(1,),
                                 padding=[(CONV_WIDTH - 1, 0)],
                                 dimension_numbers=('NWC', 'WIO', 'NWC'),
                                 feature_group_count=D_FF) + conv_b
    return (jax.nn.silu(g) * u) @ w_down


def setup_inputs(seed: int = 0) -> dict:
    key = jax.random.key(seed)
    ks = jax.random.split(key, 24)
    f32 = jnp.float32

    def nrm(k, shape, scale):
        return jax.random.normal(k, shape, f32) * scale

    def gain(k, shape):
        return 1.0 + 0.02 * jax.random.normal(k, shape, f32)

    x = jax.random.normal(ks[0], (BATCH, SEQ, D_MODEL), f32)
    offsets = jax.random.randint(ks[1], (BATCH, 1), 0, 64) * CHUNK
    positions = (offsets + jnp.arange(SEQ, dtype=jnp.int32)[None, :]).astype(jnp.int32)
    return {
        'x': x,
        'positions': positions,
        'attn_norm_g': gain(ks[2], (DEPTH, D_MODEL)),
        'ffn_norm_g': gain(ks[3], (DEPTH, D_MODEL)),
        'a_w_qkv': nrm(ks[4], (N_A_LAYERS, D_MODEL, 2 * A_QK_WIDTH + A_V_WIDTH), D_MODEL ** -0.5),
        'a_q_norm_g': gain(ks[5], (N_A_LAYERS, HEAD_DIM)),
        'a_k_norm_g': gain(ks[6], (N_A_LAYERS, HEAD_DIM)),
        'a_lambda_q1': nrm(ks[7], (N_A_LAYERS, HEAD_DIM), 0.1),
        'a_lambda_k1': nrm(ks[8], (N_A_LAYERS, HEAD_DIM), 0.1),
        'a_lambda_q2': nrm(ks[9], (N_A_LAYERS, HEAD_DIM), 0.1),
        'a_lambda_k2': nrm(ks[10], (N_A_LAYERS, HEAD_DIM), 0.1),
        'a_subln_g': gain(ks[11], (N_A_LAYERS, 2 * HEAD_DIM)),
        'a_w_o': nrm(ks[12], (N_A_LAYERS, A_V_WIDTH, D_MODEL), A_V_WIDTH ** -0.5),
        'kv_norm_g': gain(ks[13], (D_MODEL,)),
        'b_w_kv': nrm(ks[14], (D_MODEL, 2 * B_WIDTH), D_MODEL ** -0.5),
        'b_w_q': nrm(ks[15], (N_B_LAYERS, D_MODEL, B_WIDTH), D_MODEL ** -0.5),
        'b_w_o': nrm(ks[16], (N_B_LAYERS, B_WIDTH, D_MODEL), B_WIDTH ** -0.5),
        'ffn_w_up': nrm(ks[17], (DEPTH, D_MODEL, 2 * D_FF), D_MODEL ** -0.5),
        'ffn_conv_w': nrm(ks[18], (DEPTH, CONV_WIDTH, D_FF), CONV_WIDTH ** -0.5),
        'ffn_conv_b': nrm(ks[19], (DEPTH, D_FF), 0.01),
        'ffn_w_down': nrm(ks[20], (DEPTH, D_FF, D_MODEL), D_FF ** -0.5),
    }


def reference(x, positions, attn_norm_g, ffn_norm_g, a_w_qkv, a_q_norm_g, a_k_norm_g,
              a_lambda_q1, a_lambda_k1, a_lambda_q2, a_lambda_k2, a_subln_g, a_w_o,
              kv_norm_g, b_w_kv, b_w_q, b_w_o, ffn_w_up, ffn_conv_w, ffn_conv_b, ffn_w_down):
    cos, sin = _rotary_tables(positions)
    b, s, _ = x.shape
    h = x
    k_shared = None
    v_shared = None
    for layer in range(DEPTH):
        if layer < N_A_LAYERS:
            lambda_init = 0.8 - 0.6 * math.exp(-0.3 * layer)
            xn = _rmsnorm(h, attn_norm_g[layer])
            h = h + _diff_attention(xn, a_w_qkv[layer], a_q_norm_g[layer], a_k_norm_g[layer],
                                    a_lambda_q1[layer], a_lambda_k1[layer],
                                    a_lambda_q2[layer], a_lambda_k2[layer],
                                    a_subln_g[layer], a_w_o[layer], cos, sin, lambda_init)
        else:
            if layer == N_A_LAYERS:
                kv = _rmsnorm(h, kv_norm_g) @ b_w_kv
                k_shared = kv[..., :B_WIDTH].reshape(b, s, B_HEADS, HEAD_DIM)
                v_shared = kv[..., B_WIDTH:].reshape(b, s, B_HEADS, HEAD_DIM)
            j = layer - N_A_LAYERS
            xn = _rmsnorm(h, attn_norm_g[layer])
            h = h + _stick_breaking(xn, b_w_q[j], k_shared, v_shared, b_w_o[j])
        h = h + _conv_ffn(_rmsnorm(h, ffn_norm_g[layer]), ffn_w_up[layer], ffn_conv_w[layer],
                          ffn_conv_b[layer], ffn_w_down[layer])
    return h
```

```python
import functools
import math

import jax
import jax.numpy as jnp
from jax import lax
from jax.experimental import pallas as pl
from jax.experimental.pallas import tpu as pltpu

F32 = jnp.float32
BF16 = jnp.bfloat16

CHUNK = 64
HEAD_DIM = 128
ROT_DIM = HEAD_DIM // 4
ROPE_THETA = 500000.0
CONV_WIDTH = 3
EPS = 1e-6

LANES = 128
SUBLANES = 8
VMEM_LIMIT_BYTES = 56 * 1024 * 1024

BM = 1024
BN = 1024
BN_FFN = 512
BN_DOWN = 512
TQ = 256
TK = 256


def _params(*sem):
    return pltpu.CompilerParams(dimension_semantics=sem, vmem_limit_bytes=VMEM_LIMIT_BYTES)


def _rms_rows(x, gain):
    ms = jnp.mean(x * x, axis=-1, keepdims=True)
    return x * lax.rsqrt(ms + EPS) * gain


def _rotary_table_kernel(pos_ref, inv_ref, c_ref, slo_ref, shi_ref):
    ang = pos_ref[...] * inv_ref[...]
    cos = jnp.cos(ang)
    sin = jnp.sin(ang)
    lane = lax.broadcasted_iota(jnp.int32, ang.shape, 1)
    half = ROT_DIM // 2
    c_ref[...] = jnp.where(lane < ROT_DIM, cos, 1.0)
    slo_ref[...] = jnp.where(lane < half, -sin, 0.0)
    shi_ref[...] = jnp.where((lane >= half) & (lane < ROT_DIM), sin, 0.0)


def _rotary_tables(positions):
    tokens = positions.size
    bm = min(BM, tokens)
    pos = positions.reshape(tokens, 1).astype(F32)
    half = ROT_DIM // 2
    inv_freq = ROPE_THETA ** (-jnp.arange(0, ROT_DIM, 2, dtype=F32) / ROT_DIM)
    inv_lane = jnp.zeros((1, HEAD_DIM), F32).at[0, :ROT_DIM].set(jnp.tile(inv_freq, 2))
    del half
    spec = pl.BlockSpec((bm, HEAD_DIM), lambda i: (i, 0))
    out = jax.ShapeDtypeStruct((tokens, HEAD_DIM), F32)
    return pl.pallas_call(
        _rotary_table_kernel,
        grid=(tokens // bm,),
        in_specs=[pl.BlockSpec((bm, 1), lambda i: (i, 0)),
                  pl.BlockSpec((1, HEAD_DIM), lambda i: (0, 0))],
        out_specs=[spec, spec, spec],
        out_shape=[out, out, out],
        compiler_params=_params("arbitrary"),
        name="rotary_tables",
    )(pos, inv_lane)


def _norm_proj_kernel(x_ref, g_ref, w_ref, o_ref, xn_ref, *, out_scale):
    @pl.when(pl.program_id(1) == 0)
    def _():
        xn_ref[...] = _rms_rows(x_ref[...], g_ref[...]).astype(BF16)

    acc = jnp.dot(xn_ref[...], w_ref[...], preferred_element_type=F32)
    if out_scale != 1.0:
        acc = acc * out_scale
    o_ref[...] = acc.astype(o_ref.dtype)


def _norm_proj(x, gain, w, *, out_scale=1.0, name):
    tokens, d = x.shape
    n = w.shape[1]
    bm, bn = min(BM, tokens), min(BN, n)
    return pl.pallas_call(
        functools.partial(_norm_proj_kernel, out_scale=out_scale),
        grid=(tokens // bm, n // bn),
        in_specs=[pl.BlockSpec((bm, d), lambda i, j: (i, 0)),
                  pl.BlockSpec((1, d), lambda i, j: (0, 0)),
                  pl.BlockSpec((d, bn), lambda i, j: (0, j))],
        out_specs=pl.BlockSpec((bm, bn), lambda i, j: (i, j)),
        out_shape=jax.ShapeDtypeStruct((tokens, n), BF16),
        scratch_shapes=[pltpu.VMEM((bm, d), BF16)],
        compiler_params=_params("arbitrary", "arbitrary"),
        name=name,
    )(x, gain.reshape(1, d), w)


def _qkv_kernel(x_ref, g_ref, w_ref, hg_ref, c_ref, slo_ref, shi_ref, o_ref, xn_ref, *, n_qk_blocks):
    j = pl.program_id(1)

    @pl.when(j == 0)
    def _():
        xn_ref[...] = _rms_rows(x_ref[...], g_ref[...]).astype(BF16)

    acc = jnp.dot(xn_ref[...], w_ref[...], preferred_element_type=F32)
    bn = acc.shape[1]
    half = ROT_DIM // 2

    @pl.when(j < n_qk_blocks)
    def _():
        c, slo, shi = c_ref[...], slo_ref[...], shi_ref[...]
        for h in range(bn // HEAD_DIM):
            cols = slice(h * HEAD_DIM, (h + 1) * HEAD_DIM)
            y = _rms_rows(acc[:, cols], hg_ref[:, cols])
            rot = (y * c + pltpu.roll(y, HEAD_DIM - half, 1) * slo + pltpu.roll(y, half, 1) * shi)
            o_ref[:, cols] = rot.astype(o_ref.dtype)

    @pl.when(j >= n_qk_blocks)
    def _():
        o_ref[...] = acc.astype(o_ref.dtype)


def _qkv_proj(x, gain, w, head_gain, tables, *, qk_width, name):
    tokens, d = x.shape
    n = w.shape[1]
    bm, bn = min(BM, tokens), min(BN, qk_width)
    tab = pl.BlockSpec((bm, HEAD_DIM), lambda i, j: (i, 0))
    return pl.pallas_call(
        functools.partial(_qkv_kernel, n_qk_blocks=2 * qk_width // bn),
        grid=(tokens // bm, n // bn),
        in_specs=[pl.BlockSpec((bm, d), lambda i, j: (i, 0)),
                  pl.BlockSpec((1, d), lambda i, j: (0, 0)),
                  pl.BlockSpec((d, bn), lambda i, j: (0, j)),
                  pl.BlockSpec((1, bn), lambda i, j: (0, j)),
                  tab, tab, tab],
        out_specs=pl.BlockSpec((bm, bn), lambda i, j: (i, j)),
        out_shape=jax.ShapeDtypeStruct((tokens, n), BF16),
        scratch_shapes=[pltpu.VMEM((bm, d), BF16)],
        compiler_params=_params("arbitrary", "arbitrary"),
        name=name,
    )(x, gain.reshape(1, d), w, head_gain, *tables)


def _proj_residual_kernel(x_ref, w_ref, r_ref, o_ref):
    o_ref[...] = r_ref[...] + jnp.dot(x_ref[...], w_ref[...], preferred_element_type=F32)


def _proj_residual(x, w, res, *, bn, name):
    tokens, k = x.shape
    n = w.shape[1]
    bm, bn = min(BM, tokens), min(bn, n)
    return pl.pallas_call(
        _proj_residual_kernel,
        grid=(tokens // bm, n // bn),
        in_specs=[pl.BlockSpec((bm, k), lambda i, j: (i, 0)),
                  pl.BlockSpec((k, bn), lambda i, j: (0, j)),
                  pl.BlockSpec((bm, bn), lambda i, j: (i, j))],
        out_specs=pl.BlockSpec((bm, bn), lambda i, j: (i, j)),
        out_shape=jax.ShapeDtypeStruct((tokens, n), F32),
        compiler_params=_params("arbitrary", "arbitrary"),
        name=name,
    )(x, w, res)


def _ffn_up_kernel(x_ref, g_ref, wu_ref, wg_ref, cw_ref, cb_ref, o_ref, xn_ref, halo_ref, *, blocks_per_seq):
    i, j = pl.program_id(0), pl.program_id(1)

    @pl.when(j == 0)
    def _():
        xn_ref[...] = _rms_rows(x_ref[...], g_ref[...]).astype(BF16)

    xn = xn_ref[...]
    u = jnp.dot(xn, wu_ref[...], preferred_element_type=F32)
    g = jnp.dot(xn, wg_ref[...], preferred_element_type=F32)
    bm = g.shape[0]

    prev = jnp.where(i % blocks_per_seq == 0, 0.0, halo_ref[j])
    halo_ref[j] = g[bm - SUBLANES:, :]
    ext = jnp.concatenate([prev, g], axis=0)
    g1 = pltpu.roll(ext, 1, 0)[SUBLANES:, :]
    g2 = pltpu.roll(ext, 2, 0)[SUBLANES:, :]
    conv = cw_ref[2:3, :] * g + cw_ref[1:2, :] * g1 + cw_ref[0:1, :] * g2 + cb_ref[...]
    act = conv * (1.0 / (1.0 + jnp.exp(-conv))) * u
    o_ref[...] = act.astype(o_ref.dtype)


def _ffn_up(x, gain, w_up, conv_w, conv_b, *, seq, name):
    tokens, d = x.shape
    d_ff = w_up.shape[1] // 2
    bm = min(BM, seq)
    bn = min(BN_FFN, d_ff)
    nj = d_ff // bn
    return pl.pallas_call(
        functools.partial(_ffn_up_kernel, blocks_per_seq=seq // bm),
        grid=(tokens // bm, nj),
        in_specs=[pl.BlockSpec((bm, d), lambda i, j: (i, 0)),
                  pl.BlockSpec((1, d), lambda i, j: (0, 0)),
                  pl.BlockSpec((d, bn), lambda i, j: (0, j)),
                  pl.BlockSpec((d, bn), lambda i, j: (0, j + nj)),
                  pl.BlockSpec((CONV_WIDTH, bn), lambda i, j: (0, j)),
                  pl.BlockSpec((1, bn), lambda i, j: (0, j))],
        out_specs=pl.BlockSpec((bm, bn), lambda i, j: (i, j)),
        out_shape=jax.ShapeDtypeStruct((tokens, d_ff), BF16),
        scratch_shapes=[pltpu.VMEM((bm, d), BF16),
                        pltpu.VMEM((nj, SUBLANES, bn), F32)],
        compiler_params=_params("arbitrary", "arbitrary"),
        name=name,
    )(x, gain.reshape(1, d), w_up, w_up, conv_w, conv_b.reshape(1, d_ff))


def _diff_attn_kernel(lq1_ref, lk1_ref, lq2_ref, lk2_ref, sg_ref, q_ref, k_ref, v_ref, o_ref,
                      acc_ref, *, lambda_init):
    i = pl.program_id(2)
    tq = q_ref.shape[0]
    q = q_ref[...]
    acc_ref[...] = jnp.zeros_like(acc_ref)

    row = lax.broadcasted_iota(jnp.int32, (tq, TK), 0)
    col = lax.broadcasted_iota(jnp.int32, (tq, TK), 1)
    chunk_mask = (col // CHUNK) <= (row // CHUNK)

    def tile(kt, carry, masked):
        start = pl.multiple_of(kt * TK, TK)
        k = k_ref[pl.ds(start, TK), :]
        v = v_ref[pl.ds(start, TK), :]
        new = []
        for c in range(2):
            m_prev, l_prev = carry[2 * c], carry[2 * c + 1]
            cols = slice(c * HEAD_DIM, (c + 1) * HEAD_DIM)
            s = lax.dot_general(q[:, cols], k[:, cols], (((1,), (1,)), ((), ())),
                                preferred_element_type=F32)
            if masked:
                s = jnp.where(chunk_mask, s, -jnp.inf)
            m_new = jnp.maximum(m_prev, jnp.max(s, axis=1, keepdims=True))
            alpha = jnp.exp(m_prev - m_new)
            p = jnp.exp(s - m_new)
            l_new = alpha * l_prev + jnp.sum(p, axis=1, keepdims=True)
            acc_ref[c] = alpha * acc_ref[c] + jnp.dot(p.astype(BF16), v, preferred_element_type=F32)
            new += [m_new, l_new]
        return tuple(new)

    neg = jnp.full((tq, 1), -jnp.inf, F32)
    zero = jnp.zeros((tq, 1), F32)
    carry = lax.fori_loop(0, i, lambda kt, cr: tile(kt, cr, False), (neg, zero, neg, zero))
    _, l0, _, l1 = tile(i, carry, True)

    lam = (jnp.exp(jnp.sum(lq1_ref[...] * lk1_ref[...], axis=1, keepdims=True))
           - jnp.exp(jnp.sum(lq2_ref[...] * lk2_ref[...], axis=1, keepdims=True)) + lambda_init)
    o = acc_ref[0] * (1.0 / l0) - lam * (acc_ref[1] * (1.0 / l1))
    o = _rms_rows(o, sg_ref[...]) * (1.0 - lambda_init)
    o_ref[...] = o.astype(o_ref.dtype)


def _diff_attention(qkv, lq1, lk1, lq2, lk2, subln_g, *, batch, seq, heads, lambda_init, name):
    width = 2 * HEAD_DIM
    qkv3 = qkv.reshape(batch, seq, 3 * heads * width)
    vec = pl.BlockSpec((1, HEAD_DIM), lambda b, h, i: (0, 0))
    out = pl.pallas_call(
        functools.partial(_diff_attn_kernel, lambda_init=lambda_init),
        grid=(batch, heads, seq // TQ),
        in_specs=[vec, vec, vec, vec,
                  pl.BlockSpec((1, width), lambda b, h, i: (0, 0)),
                  pl.BlockSpec((None, TQ, width), lambda b, h, i: (b, i, h)),
                  pl.BlockSpec((None, seq, width), lambda b, h, i: (b, 0, heads + h)),
                  pl.BlockSpec((None, seq, width), lambda b, h, i: (b, 0, 2 * heads + h))],
        out_specs=pl.BlockSpec((None, TQ, width), lambda b, h, i: (b, i, h)),
        out_shape=jax.ShapeDtypeStruct((batch, seq, heads * width), BF16),
        scratch_shapes=[pltpu.VMEM((2, TQ, width), F32)],
        compiler_params=_params("arbitrary", "arbitrary", "arbitrary"),
        name=name,
    )(lq1.reshape(1, -1), lk1.reshape(1, -1), lq2.reshape(1, -1), lk2.reshape(1, -1),
      subln_g.reshape(1, -1), qkv3, qkv3, qkv3)
    return out.reshape(batch * seq, heads * width)


def _stick_kernel(tri_ref, q_ref, k_ref, v_ref, o_ref):
    i = pl.program_id(2)
    tq = q_ref.shape[0]
    q = q_ref[...]
    tri = tri_ref[...]

    row = lax.broadcasted_iota(jnp.int32, (tq, TK), 0)
    col = lax.broadcasted_iota(jnp.int32, (tq, TK), 1)
    causal = col < row

    def tile(kt, carry, masked):
        acc, right = carry
        start = pl.multiple_of(kt * TK, TK)
        k = k_ref[pl.ds(start, TK), :]
        v = v_ref[pl.ds(start, TK), :]
        z = lax.dot_general(q, k, (((1,), (1,)), ((), ())), preferred_element_type=F32)
        softplus = jnp.maximum(z, 0.0) + jnp.log(1.0 + jnp.exp(-jnp.abs(z)))
        log_keep = -softplus
        if masked:
            log_keep = jnp.where(causal, log_keep, 0.0)
        hi = log_keep.astype(BF16)
        lo = (log_keep - hi.astype(F32)).astype(BF16)
        suffix = jnp.dot(jnp.concatenate([hi, lo], axis=1), tri, preferred_element_type=F32)
        log_a = (z - softplus) + suffix + right
        a = jnp.exp(log_a)
        if masked:
            a = jnp.where(causal, a, 0.0)
        acc = acc + jnp.dot(a.astype(BF16), v, preferred_element_type=F32)
        right = right + suffix[:, 0:1] + log_keep[:, 0:1]
        return acc, right

    carry = tile(i, (jnp.zeros((tq, HEAD_DIM), F32), jnp.zeros((tq, 1), F32)), True)
    acc, _ = lax.fori_loop(0, i, lambda s, cr: tile(i - 1 - s, cr, False), carry)
    o_ref[...] = acc.astype(o_ref.dtype)


def _stick_breaking(q, kv, *, batch, seq, heads, name):
    width = heads * HEAD_DIM
    q3 = q.reshape(batch, seq, width)
    kv3 = kv.reshape(batch, seq, 2 * width)
    s_idx = lax.broadcasted_iota(jnp.int32, (TK, TK), 0)
    j_idx = lax.broadcasted_iota(jnp.int32, (TK, TK), 1)
    tri = (s_idx > j_idx).astype(BF16)
    tri2 = jnp.concatenate([tri, tri], axis=0)
    out = pl.pallas_call(
        _stick_kernel,
        grid=(batch, heads, seq // TQ),
        in_specs=[pl.BlockSpec((2 * TK, TK), lambda b, h, i: (0, 0)),
                  pl.BlockSpec((None, TQ, HEAD_DIM), lambda b, h, i: (b, i, h)),
                  pl.BlockSpec((None, seq, HEAD_DIM), lambda b, h, i: (b, 0, h)),
                  pl.BlockSpec((None, seq, HEAD_DIM), lambda b, h, i: (b, 0, heads + h))],
        out_specs=pl.BlockSpec((None, TQ, HEAD_DIM), lambda b, h, i: (b, i, h)),
        out_shape=jax.ShapeDtypeStruct((batch, seq, width), BF16),
        compiler_params=_params("arbitrary", "arbitrary", "arbitrary"),
        name=name,
    )(tri2, q3, kv3, kv3)
    return out.reshape(batch * seq, width)


def kernel(x, positions, attn_norm_g, ffn_norm_g, a_w_qkv, a_q_norm_g, a_k_norm_g, a_lambda_q1, a_lambda_k1, a_lambda_q2, a_lambda_k2, a_subln_g, a_w_o, kv_norm_g, b_w_kv, b_w_q, b_w_o, ffn_w_up, ffn_conv_w, ffn_conv_b, ffn_w_down):
    batch, seq, d_model = x.shape
    depth = attn_norm_g.shape[0]
    n_a = a_w_qkv.shape[0]
    a_heads = d_model // (2 * HEAD_DIM)
    b_heads = d_model // HEAD_DIM
    qk_width = a_heads * 2 * HEAD_DIM
    scale = HEAD_DIM ** -0.5

    h = x.reshape(batch * seq, d_model)
    tables = _rotary_tables(positions)
    kv = None
    for layer in range(depth):
        if layer < n_a:
            lambda_init = 0.8 - 0.6 * math.exp(-0.3 * layer)
            head_gain = jnp.concatenate([
                jnp.tile(a_q_norm_g[layer] * scale, qk_width // HEAD_DIM),
                jnp.tile(a_k_norm_g[layer], qk_width // HEAD_DIM),
                jnp.ones((a_w_qkv.shape[2] - 2 * qk_width,), F32)]).reshape(1, -1)
            qkv = _qkv_proj(h, attn_norm_g[layer], a_w_qkv[layer].astype(BF16), head_gain, tables,
                            qk_width=qk_width, name=f"a{layer}_qkv")
            o = _diff_attention(qkv, a_lambda_q1[layer], a_lambda_k1[layer], a_lambda_q2[layer],
                                a_lambda_k2[layer], a_subln_g[layer], batch=batch, seq=seq,
                                heads=a_heads, lambda_init=lambda_init, name=f"a{layer}_attn")
            h = _proj_residual(o, a_w_o[layer].astype(BF16), h, bn=BN, name=f"a{layer}_out")
        else:
            if layer == n_a:
                kv = _norm_proj(h, kv_norm_g, b_w_kv.astype(BF16), name="b_kv")
            j = layer - n_a
            q = _norm_proj(h, attn_norm_g[layer], b_w_q[j].astype(BF16), out_scale=scale,
                           name=f"b{j}_q")
            o = _stick_breaking(q, kv, batch=batch, seq=seq, heads=b_heads, name=f"b{j}_attn")
            h = _proj_residual(o, b_w_o[j].astype(BF16), h, bn=BN, name=f"b{j}_out")
        act = _ffn_up(h, ffn_norm_g[layer], ffn_w_up[layer].astype(BF16), ffn_conv_w[layer],
                      ffn_conv_b[layer], seq=seq, name=f"l{layer}_ffn_up")
        h = _proj_residual(act, ffn_w_down[layer].astype(BF16), h, bn=BN_DOWN, name=f"l{layer}_ffn_down")
    return h.reshape(batch, seq, d_model)
```

```python
import functools
import math

import jax
import jax.numpy as jnp
from jax import lax
from jax.experimental import pallas as pl
from jax.experimental.pallas import tpu as pltpu

F32 = jnp.float32
BF16 = jnp.bfloat16

CHUNK = 64
HEAD_DIM = 128
ROT_DIM = HEAD_DIM // 4
ROPE_THETA = 500000.0
CONV_WIDTH = 3
EPS = 1e-6

LANES = 128
SUBLANES = 8
VMEM_LIMIT_BYTES = 56 * 1024 * 1024

BM = 1024
BN = 1024
BN_FFN = 512
BN_DOWN = 512
TQ = 256
TK = 256
DIFF_HEADS_PER_STEP = 4
STICK_HEADS_PER_STEP = 8
LOOKAHEAD = 3


def _params(*sem):
    return pltpu.CompilerParams(dimension_semantics=sem, vmem_limit_bytes=VMEM_LIMIT_BYTES)


def _rms_rows(x, gain):
    ms = jnp.mean(x * x, axis=-1, keepdims=True)
    return x * lax.rsqrt(ms + EPS) * gain


def _rotary_table_kernel(pos_ref, inv_ref, c_ref, slo_ref, shi_ref):
    ang = pos_ref[...] * inv_ref[...]
    cos = jnp.cos(ang)
    sin = jnp.sin(ang)
    lane = lax.broadcasted_iota(jnp.int32, ang.shape, 1)
    half = ROT_DIM // 2
    c_ref[...] = jnp.where(lane < ROT_DIM, cos, 1.0)
    slo_ref[...] = jnp.where(lane < half, -sin, 0.0)
    shi_ref[...] = jnp.where((lane >= half) & (lane < ROT_DIM), sin, 0.0)


def _rotary_tables(positions):
    tokens = positions.size
    bm = min(BM, tokens)
    pos = positions.reshape(tokens, 1).astype(F32)
    inv_freq = ROPE_THETA ** (-jnp.arange(0, ROT_DIM, 2, dtype=F32) / ROT_DIM)
    inv_lane = jnp.zeros((1, HEAD_DIM), F32).at[0, :ROT_DIM].set(jnp.tile(inv_freq, 2))
    spec = pl.BlockSpec((bm, HEAD_DIM), lambda i: (i, 0))
    out = jax.ShapeDtypeStruct((tokens, HEAD_DIM), F32)
    return pl.pallas_call(
        _rotary_table_kernel,
        grid=(tokens // bm,),
        in_specs=[pl.BlockSpec((bm, 1), lambda i: (i, 0)),
                  pl.BlockSpec((1, HEAD_DIM), lambda i: (0, 0))],
        out_specs=[spec, spec, spec],
        out_shape=[out, out, out],
        compiler_params=_params("arbitrary"),
        name="rotary_tables",
    )(pos, inv_lane)


def _norm_proj_kernel(x_ref, g_ref, w_ref, o_ref, xn_ref, *, out_scale):
    @pl.when(pl.program_id(1) == 0)
    def _():
        xn_ref[...] = _rms_rows(x_ref[...], g_ref[...]).astype(BF16)

    acc = jnp.dot(xn_ref[...], w_ref[...], preferred_element_type=F32)
    if out_scale != 1.0:
        acc = acc * out_scale
    o_ref[...] = acc.astype(o_ref.dtype)


def _norm_proj(x, gain, w, *, out_scale=1.0, name):
    tokens, d = x.shape
    n = w.shape[1]
    bm, bn = min(BM, tokens), min(BN, n)
    assert tokens % bm == 0 and n % bn == 0, (tokens, bm, n, bn)
    return pl.pallas_call(
        functools.partial(_norm_proj_kernel, out_scale=out_scale),
        grid=(tokens // bm, n // bn),
        in_specs=[pl.BlockSpec((bm, d), lambda i, j: (i, 0)),
                  pl.BlockSpec((1, d), lambda i, j: (0, 0)),
                  pl.BlockSpec((d, bn), lambda i, j: (0, j))],
        out_specs=pl.BlockSpec((bm, bn), lambda i, j: (i, j)),
        out_shape=jax.ShapeDtypeStruct((tokens, n), BF16),
        scratch_shapes=[pltpu.VMEM((bm, d), BF16)],
        compiler_params=_params("arbitrary", "arbitrary"),
        name=name,
    )(x, gain.reshape(1, d), w)


def _qkv_kernel(x_ref, g_ref, w_ref, hg_ref, c_ref, slo_ref, shi_ref, o_ref, xn_ref, *, n_qk_blocks):
    j = pl.program_id(1)

    @pl.when(j == 0)
    def _():
        xn_ref[...] = _rms_rows(x_ref[...], g_ref[...]).astype(BF16)

    acc = jnp.dot(xn_ref[...], w_ref[...], preferred_element_type=F32)
    bn = acc.shape[1]
    half = ROT_DIM // 2

    @pl.when(j < n_qk_blocks)
    def _():
        c, slo, shi = c_ref[...], slo_ref[...], shi_ref[...]
        for h in range(bn // HEAD_DIM):
            cols = slice(h * HEAD_DIM, (h + 1) * HEAD_DIM)
            y = _rms_rows(acc[:, cols], hg_ref[:, cols])
            rot = (y * c + pltpu.roll(y, HEAD_DIM - half, 1) * slo + pltpu.roll(y, half, 1) * shi)
            o_ref[:, cols] = rot.astype(o_ref.dtype)

    @pl.when(j >= n_qk_blocks)
    def _():
        o_ref[...] = acc.astype(o_ref.dtype)


def _qkv_proj(x, gain, w, head_gain, tables, *, qk_width, name):
    tokens, d = x.shape
    n = w.shape[1]
    bm, bn = min(BM, tokens), min(BN, qk_width)
    assert tokens % bm == 0 and qk_width % bn == 0 and n % bn == 0, (tokens, bm, qk_width, n, bn)
    tab = pl.BlockSpec((bm, HEAD_DIM), lambda i, j: (i, 0))
    return pl.pallas_call(
        functools.partial(_qkv_kernel, n_qk_blocks=2 * qk_width // bn),
        grid=(tokens // bm, n // bn),
        in_specs=[pl.BlockSpec((bm, d), lambda i, j: (i, 0)),
                  pl.BlockSpec((1, d), lambda i, j: (0, 0)),
                  pl.BlockSpec((d, bn), lambda i, j: (0, j)),
                  pl.BlockSpec((1, bn), lambda i, j: (0, j)),
                  tab, tab, tab],
        out_specs=pl.BlockSpec((bm, bn), lambda i, j: (i, j)),
        out_shape=jax.ShapeDtypeStruct((tokens, n), BF16),
        scratch_shapes=[pltpu.VMEM((bm, d), BF16)],
        compiler_params=_params("arbitrary", "arbitrary"),
        name=name,
    )(x, gain.reshape(1, d), w, head_gain, *tables)


def _proj_residual_kernel(x_ref, w_ref, r_ref, o_ref):
    o_ref[...] = r_ref[...] + jnp.dot(x_ref[...], w_ref[...], preferred_element_type=F32)


def _proj_residual(x, w, res, *, bn, name):
    tokens, k = x.shape
    n = w.shape[1]
    bm, bn = min(BM, tokens), min(bn, n)
    assert tokens % bm == 0 and n % bn == 0, (tokens, bm, n, bn)
    return pl.pallas_call(
        _proj_residual_kernel,
        grid=(tokens // bm, n // bn),
        in_specs=[pl.BlockSpec((bm, k), lambda i, j: (i, 0)),
                  pl.BlockSpec((k, bn), lambda i, j: (0, j)),
                  pl.BlockSpec((bm, bn), lambda i, j: (i, j))],
        out_specs=pl.BlockSpec((bm, bn), lambda i, j: (i, j)),
        out_shape=jax.ShapeDtypeStruct((tokens, n), F32),
        compiler_params=_params("arbitrary", "arbitrary"),
        name=name,
    )(x, w, res)


def _ffn_up_kernel(x_ref, g_ref, wu_ref, wg_ref, cw_ref, cb_ref, o_ref, xn_ref, halo_ref, *, blocks_per_seq):
    i, j = pl.program_id(0), pl.program_id(1)

    @pl.when(j == 0)
    def _():
        xn_ref[...] = _rms_rows(x_ref[...], g_ref[...]).astype(BF16)

    xn = xn_ref[...]
    u = jnp.dot(xn, wu_ref[...], preferred_element_type=F32)
    g = jnp.dot(xn, wg_ref[...], preferred_element_type=F32)
    bm = g.shape[0]

    prev = jnp.where(i % blocks_per_seq == 0, 0.0, halo_ref[j])
    halo_ref[j] = g[bm - SUBLANES:, :]
    ext = jnp.concatenate([prev, g], axis=0)
    g1 = pltpu.roll(ext, 1, 0)[SUBLANES:, :]
    g2 = pltpu.roll(ext, 2, 0)[SUBLANES:, :]
    conv = cw_ref[2:3, :] * g + cw_ref[1:2, :] * g1 + cw_ref[0:1, :] * g2 + cb_ref[...]
    act = conv * (1.0 / (1.0 + jnp.exp(-conv))) * u
    o_ref[...] = act.astype(o_ref.dtype)


def _ffn_up(x, gain, w_up, conv_w, conv_b, *, seq, name):
    tokens, d = x.shape
    d_ff = w_up.shape[1] // 2
    bm = min(BM, seq)
    bn = min(BN_FFN, d_ff)
    assert tokens % bm == 0 and d_ff % bn == 0, (tokens, bm, d_ff, bn)
    nj = d_ff // bn
    return pl.pallas_call(
        functools.partial(_ffn_up_kernel, blocks_per_seq=seq // bm),
        grid=(tokens // bm, nj),
        in_specs=[pl.BlockSpec((bm, d), lambda i, j: (i, 0)),
                  pl.BlockSpec((1, d), lambda i, j: (0, 0)),
                  pl.BlockSpec((d, bn), lambda i, j: (0, j)),
                  pl.BlockSpec((d, bn), lambda i, j: (0, j + nj)),
                  pl.BlockSpec((CONV_WIDTH, bn), lambda i, j: (0, j)),
                  pl.BlockSpec((1, bn), lambda i, j: (0, j))],
        out_specs=pl.BlockSpec((bm, bn), lambda i, j: (i, j)),
        out_shape=jax.ShapeDtypeStruct((tokens, d_ff), BF16),
        scratch_shapes=[pltpu.VMEM((bm, d), BF16),
                        pltpu.VMEM((nj, SUBLANES, bn), F32)],
        compiler_params=_params("arbitrary", "arbitrary"),
        name=name,
    )(x, gain.reshape(1, d), w_up, w_up, conv_w, conv_b.reshape(1, d_ff))


def _all_sublanes(x, op):
    for d in (1, 2, 4):
        x = op(x, pltpu.roll(x, d, 0))
    return x


def _transposed_tiles(v3):
    b, s, w = v3.shape
    return v3.reshape(b, s // TK, TK, w).swapaxes(2, 3)


def _diff_attn_kernel(lq1_ref, lk1_ref, lq2_ref, lk2_ref, sg_ref, q_ref, k_ref, vt_ref, o_ref,
                      acc_ref, qt_ref, sc_ref, pc_ref, ac_ref, *, lambda_init, heads_per_step, lookahead):
    i = pl.program_id(2)
    tq = q_ref.shape[0]
    nv = TK // SUBLANES
    width = 2 * HEAD_DIM
    units = 2 * heads_per_step

    row = lax.broadcasted_iota(jnp.int32, (TK, tq), 0)
    col = lax.broadcasted_iota(jnp.int32, (TK, tq), 1)
    chunk_mask = (row // CHUNK) <= (col // CHUNK)

    acc_ref[...] = jnp.zeros_like(acc_ref)
    for u in range(units):
        qt_ref[u] = q_ref[:, u * HEAD_DIM:(u + 1) * HEAD_DIM].T

    def scores(kt, u):
        start = pl.multiple_of(kt * TK, TK)
        k = k_ref[pl.ds(start, TK), u * HEAD_DIM:(u + 1) * HEAD_DIM]
        return jnp.dot(k, qt_ref[u], preferred_element_type=F32)

    def softmax_step(s, m, lpart, masked):
        if masked:
            s = jnp.where(chunk_mask, s, -jnp.inf)
        p_cols, m_cols, l_cols, a_cols = [], [], [], []
        for ch in range(tq // LANES):
            lanes = slice(ch * LANES, (ch + 1) * LANES)
            s3 = s[:, lanes].reshape(nv, SUBLANES, LANES)
            tmax = s3[0]
            for v in range(1, nv):
                tmax = jnp.maximum(tmax, s3[v])
            m_new = jnp.maximum(m[:, lanes], _all_sublanes(tmax, jnp.maximum))
            alpha = jnp.exp2(m[:, lanes] - m_new)
            p3 = [jnp.exp2(s3[v] - m_new) for v in range(nv)]
            psum = p3[0]
            for v in range(1, nv):
                psum = psum + p3[v]
            p_cols.append(jnp.concatenate(p3, axis=0))
            m_cols.append(m_new)
            l_cols.append(alpha * lpart[:, lanes] + psum)
            a_cols.append(alpha)
        cat = lambda xs: jnp.concatenate(xs, axis=1)
        return cat(p_cols).astype(BF16), cat(m_cols), cat(l_cols), cat(a_cols)

    def rows(x):
        return jnp.concatenate([x] * (width // SUBLANES), axis=0)

    def accumulate(u, kt, alpha, p):
        hd = u // 2
        acc_ref[u] = rows(alpha) * acc_ref[u] + jnp.dot(vt_ref[kt, hd * width:(hd + 1) * width, :], p,
                                                        preferred_element_type=F32)

    def tile(kt, kt_next, carry, masked, first):
        if not first:
            accumulate(units - 1, kt + 1, ac_ref[...], pc_ref[...])
        new = []
        ahead = [sc_ref[d] for d in range(lookahead)]
        for u in range(units):
            s = ahead.pop(0)
            ahead.append(scores(kt, u + lookahead) if u + lookahead < units
                         else scores(kt_next, u + lookahead - units))
            p, m, l, a = softmax_step(s, carry[2 * u], carry[2 * u + 1], masked)
            new += [m, l]
            if u + 1 < units:
                accumulate(u, kt, a, p)
            else:
                pc_ref[...] = p
                ac_ref[...] = a
        for d in range(lookahead):
            sc_ref[d] = ahead[d]
        return tuple(new)

    for d in range(lookahead):
        sc_ref[d] = scores(i, d)
    neg = jnp.full((SUBLANES, tq), -jnp.inf, F32)
    zero = jnp.zeros((SUBLANES, tq), F32)
    carry = tile(i, jnp.maximum(i - 1, 0), (neg, zero) * units, True, True)
    carry = lax.fori_loop(
        0, i, lambda s, cr: tile(i - 1 - s, jnp.maximum(i - 2 - s, 0), cr, False, False), carry)
    accumulate(units - 1, 0, ac_ref[...], pc_ref[...])

    lam = (jnp.exp(jnp.sum(lq1_ref[...] * lk1_ref[...], axis=1, keepdims=True))
           - jnp.exp(jnp.sum(lq2_ref[...] * lk2_ref[...], axis=1, keepdims=True)) + lambda_init)
    for hd in range(heads_per_step):
        inv0 = 1.0 / _all_sublanes(carry[4 * hd + 1], jnp.add)
        inv1 = 1.0 / _all_sublanes(carry[4 * hd + 3], jnp.add)
        o_t = acc_ref[2 * hd] * rows(inv0) - lam * (acc_ref[2 * hd + 1] * rows(inv1))
        o = _rms_rows(o_t.T, sg_ref[...]) * (1.0 - lambda_init)
        o_ref[:, hd * width:(hd + 1) * width] = o.astype(o_ref.dtype)


def _diff_attention(qkv, lq1, lk1, lq2, lk2, subln_g, *, batch, seq, heads, lambda_init, name):
    width = 2 * HEAD_DIM
    hp = min(DIFF_HEADS_PER_STEP, heads)
    groups = heads // hp
    qkv3 = qkv.reshape(batch, seq, 3 * heads * width)
    vt = _transposed_tiles(qkv3[..., 2 * heads * width:])
    vec = pl.BlockSpec((1, HEAD_DIM), lambda b, h, i: (0, 0))
    out = pl.pallas_call(
        functools.partial(_diff_attn_kernel, lambda_init=lambda_init, heads_per_step=hp,
                          lookahead=min(LOOKAHEAD, 2 * hp)),
        grid=(batch, groups, seq // TQ),
        in_specs=[vec, vec, vec, vec,
                  pl.BlockSpec((1, width), lambda b, h, i: (0, 0)),
                  pl.BlockSpec((None, TQ, hp * width), lambda b, h, i: (b, i, h)),
                  pl.BlockSpec((None, seq, hp * width), lambda b, h, i: (b, 0, groups + h)),
                  pl.BlockSpec((None, seq // TK, hp * width, TK), lambda b, h, i: (b, 0, h, 0))],
        out_specs=pl.BlockSpec((None, TQ, hp * width), lambda b, h, i: (b, i, h)),
        out_shape=jax.ShapeDtypeStruct((batch, seq, heads * width), BF16),
        scratch_shapes=[pltpu.VMEM((2 * hp, width, TQ), F32),
                        pltpu.VMEM((2 * hp, HEAD_DIM, TQ), BF16),
                        pltpu.VMEM((LOOKAHEAD, TK, TQ), F32),
                        pltpu.VMEM((TK, TQ), BF16),
                        pltpu.VMEM((SUBLANES, TQ), F32)],
        compiler_params=_params("arbitrary", "arbitrary", "arbitrary"),
        name=name,
    )(lq1.reshape(1, -1), lk1.reshape(1, -1), lq2.reshape(1, -1), lk2.reshape(1, -1),
      subln_g.reshape(1, -1), qkv3, qkv3, vt)
    return out.reshape(batch * seq, heads * width)


def _stick_kernel(q_ref, k_ref, vt_ref, o_ref, acc_ref, qt_ref, zc_ref, ac_ref, *, heads_per_step, lookahead):
    i = pl.program_id(2)
    tq = q_ref.shape[0]
    nv = TK // SUBLANES
    hp = heads_per_step
    last = slice((hp - 1) * HEAD_DIM, hp * HEAD_DIM)

    row = lax.broadcasted_iota(jnp.int32, (TK, tq), 0)
    col = lax.broadcasted_iota(jnp.int32, (TK, tq), 1)
    causal = ((row % SUBLANES) * nv + row // SUBLANES) < col
    sub = lax.broadcasted_iota(jnp.int32, (SUBLANES, LANES), 0)

    acc_ref[...] = jnp.zeros_like(acc_ref)
    for hh in range(hp):
        qt_ref[hh] = q_ref[:, hh * HEAD_DIM:(hh + 1) * HEAD_DIM].T

    def scores(kt, hh):
        start = pl.multiple_of(kt * TK, TK)
        k = k_ref[pl.ds(start, TK), hh * HEAD_DIM:(hh + 1) * HEAD_DIM]
        return jnp.dot(k, qt_ref[hh], preferred_element_type=F32)

    def weights(zh, right, masked):
        beta = 0.5 * jnp.tanh(zh) + 0.5
        if masked:
            beta = jnp.where(causal, beta, 0.0)
        a_cols, totals = [], []
        for ch in range(tq // LANES):
            lanes = slice(ch * LANES, (ch + 1) * LANES)
            beta3 = beta[:, lanes].reshape(nv, SUBLANES, LANES)
            run = jnp.ones((SUBLANES, LANES), F32)
            part = [None] * nv
            for v in range(nv - 1, -1, -1):
                part[v] = beta3[v] * run
                run = run - part[v]
            t = run
            for d in (1, 2, 4):
                shifted = pltpu.roll(t, SUBLANES - d, 0)
                t = t * jnp.where(sub + d < SUBLANES, shifted, 1.0)
            later = jnp.where(sub + 1 < SUBLANES, pltpu.roll(t, SUBLANES - 1, 0), 1.0)
            off = right[:, lanes] * later
            a_cols.append(jnp.concatenate([part[v] * off for v in range(nv)], axis=0))
            totals.append(jnp.broadcast_to(t[0:1, :], (SUBLANES, LANES)))
        a = jnp.concatenate(a_cols, axis=1).astype(BF16)
        return a, right * jnp.concatenate(totals, axis=1)

    def tile(kt, kt_next, right, masked, first):
        if not first:
            acc_ref[hp - 1] += jnp.dot(vt_ref[kt + 1, last, :], ac_ref[...], preferred_element_type=F32)
        new_right = []
        ahead = [zc_ref[d] for d in range(lookahead)]
        for hh in range(hp):
            z = ahead.pop(0)
            ahead.append(scores(kt, hh + lookahead) if hh + lookahead < hp
                         else scores(kt_next, hh + lookahead - hp))
            a, r = weights(z, right[hh], masked)
            new_right.append(r)
            if hh + 1 < hp:
                acc_ref[hh] += jnp.dot(vt_ref[kt, hh * HEAD_DIM:(hh + 1) * HEAD_DIM, :], a,
                                       preferred_element_type=F32)
            else:
                ac_ref[...] = a
        for d in range(lookahead):
            zc_ref[d] = ahead[d]
        return tuple(new_right)

    for d in range(lookahead):
        zc_ref[d] = scores(i, d)
    ones = tuple(jnp.ones((SUBLANES, tq), F32) for _ in range(hp))
    right = tile(i, jnp.maximum(i - 1, 0), ones, True, True)
    lax.fori_loop(0, i, lambda s, rt: tile(i - 1 - s, jnp.maximum(i - 2 - s, 0), rt, False, False), right)
    acc_ref[hp - 1] += jnp.dot(vt_ref[0, last, :], ac_ref[...], preferred_element_type=F32)
    for hh in range(hp):
        o_ref[:, hh * HEAD_DIM:(hh + 1) * HEAD_DIM] = acc_ref[hh].T.astype(o_ref.dtype)


def _permute_keys(k3, v3):
    b, s, w = k3.shape
    nv = TK // SUBLANES
    kp = k3.reshape(b, s // TK, SUBLANES, nv, w).swapaxes(2, 3).reshape(b, s, w)
    vt = v3.reshape(b, s // TK, SUBLANES, nv, w).transpose(0, 1, 4, 3, 2).reshape(b, s // TK, w, TK)
    return kp, vt


def _stick_breaking(q, kp, vt, *, batch, seq, heads, name):
    width = heads * HEAD_DIM
    hp = min(STICK_HEADS_PER_STEP, heads)
    q3 = q.reshape(batch, seq, width)
    out = pl.pallas_call(
        functools.partial(_stick_kernel, heads_per_step=hp, lookahead=min(LOOKAHEAD, hp)),
        grid=(batch, heads // hp, seq // TQ),
        in_specs=[pl.BlockSpec((None, TQ, hp * HEAD_DIM), lambda b, h, i: (b, i, h)),
                  pl.BlockSpec((None, seq, hp * HEAD_DIM), lambda b, h, i: (b, 0, h)),
                  pl.BlockSpec((None, seq // TK, hp * HEAD_DIM, TK), lambda b, h, i: (b, 0, h, 0))],
        out_specs=pl.BlockSpec((None, TQ, hp * HEAD_DIM), lambda b, h, i: (b, i, h)),
        out_shape=jax.ShapeDtypeStruct((batch, seq, width), BF16),
        scratch_shapes=[pltpu.VMEM((hp, HEAD_DIM, TQ), F32),
                        pltpu.VMEM((hp, HEAD_DIM, TQ), BF16),
                        pltpu.VMEM((LOOKAHEAD, TK, TQ), F32),
                        pltpu.VMEM((TK, TQ), BF16)],
        compiler_params=_params("arbitrary", "arbitrary", "arbitrary"),
        name=name,
    )(q3, kp, vt)
    return out.reshape(batch * seq, width)


def kernel(x, positions, attn_norm_g, ffn_norm_g, a_w_qkv, a_q_norm_g, a_k_norm_g, a_lambda_q1, a_lambda_k1, a_lambda_q2, a_lambda_k2, a_subln_g, a_w_o, kv_norm_g, b_w_kv, b_w_q, b_w_o, ffn_w_up, ffn_conv_w, ffn_conv_b, ffn_w_down):
    batch, seq, d_model = x.shape
    depth = attn_norm_g.shape[0]
    n_a = a_w_qkv.shape[0]
    a_heads = d_model // (2 * HEAD_DIM)
    b_heads = d_model // HEAD_DIM
    qk_width = a_heads * 2 * HEAD_DIM
    scale = HEAD_DIM ** -0.5

    h = x.reshape(batch * seq, d_model)
    tables = _rotary_tables(positions)
    kp = vt = None
    for layer in range(depth):
        if layer < n_a:
            lambda_init = 0.8 - 0.6 * math.exp(-0.3 * layer)
            head_gain = jnp.concatenate([
                jnp.tile(a_q_norm_g[layer] * (scale * math.log2(math.e)), qk_width // HEAD_DIM),
                jnp.tile(a_k_norm_g[layer], qk_width // HEAD_DIM),
                jnp.ones((a_w_qkv.shape[2] - 2 * qk_width,), F32)]).reshape(1, -1)
            qkv = _qkv_proj(h, attn_norm_g[layer], a_w_qkv[layer].astype(BF16), head_gain, tables,
                            qk_width=qk_width, name=f"a{layer}_qkv")
            o = _diff_attention(qkv, a_lambda_q1[layer], a_lambda_k1[layer], a_lambda_q2[layer],
                                a_lambda_k2[layer], a_subln_g[layer], batch=batch, seq=seq,
                                heads=a_heads, lambda_init=lambda_init, name=f"a{layer}_attn")
            h = _proj_residual(o, a_w_o[layer].astype(BF16), h, bn=BN, name=f"a{layer}_out")
        else:
            if layer == n_a:
                kv = _norm_proj(h, kv_norm_g, b_w_kv.astype(BF16), name="b_kv")
                kv = kv.reshape(batch, seq, 2 * b_heads * HEAD_DIM)
                kp, vt = _permute_keys(kv[..., :b_heads * HEAD_DIM], kv[..., b_heads * HEAD_DIM:])
            j = layer - n_a
            q = _norm_proj(h, attn_norm_g[layer], b_w_q[j].astype(BF16), out_scale=0.5 * scale,
                           name=f"b{j}_q")
            o = _stick_breaking(q, kp, vt, batch=batch, seq=seq, heads=b_heads, name=f"b{j}_attn")
            h = _proj_residual(o, b_w_o[j].astype(BF16), h, bn=BN, name=f"b{j}_out")
        act = _ffn_up(h, ffn_norm_g[layer], ffn_w_up[layer].astype(BF16), ffn_conv_w[layer],
                      ffn_conv_b[layer], seq=seq, name=f"l{layer}_ffn_up")
        h = _proj_residual(act, ffn_w_down[layer].astype(BF16), h, bn=BN_DOWN, name=f"l{layer}_ffn_down")
    return h.reshape(batch, seq, d_model)
```

```python
import functools
import math

import jax
import jax.numpy as jnp
from jax import lax
from jax.experimental import pallas as pl
from jax.experimental.pallas import tpu as pltpu

F32 = jnp.float32
BF16 = jnp.bfloat16

CHUNK = 64
HEAD_DIM = 128
ROT_DIM = HEAD_DIM // 4
ROT_PARTNER = HEAD_DIM // 2
ROPE_THETA = 500000.0
CONV_WIDTH = 3
EPS = 1e-6

LANES = 128
SUBLANES = 8
VMEM_LIMIT_BYTES = 56 * 1024 * 1024

BM = 1024
BN = 1024
BN_FFN = 512
BN_DOWN = 512
TQ = 256
TK = 256
DIFF_HEADS_PER_STEP = 8
STICK_HEADS_PER_STEP = 16
LOOKAHEAD = 3


def _params(*sem):
    return pltpu.CompilerParams(dimension_semantics=sem, vmem_limit_bytes=VMEM_LIMIT_BYTES)


def _rms_rows(x, gain):
    ms = jnp.mean(x * x, axis=-1, keepdims=True)
    return x * lax.rsqrt(ms + EPS) * gain


def _head_dim_order():
    half = ROT_DIM // 2
    return jnp.concatenate([jnp.arange(0, half), jnp.arange(ROT_DIM, ROT_PARTNER + half),
                            jnp.arange(half, ROT_DIM), jnp.arange(ROT_PARTNER + half, HEAD_DIM)])


def _rotary_table_kernel(pos_ref, inv_ref, c_ref, s_ref):
    ang = pos_ref[...] * inv_ref[...]
    cos = jnp.cos(ang)
    sin = jnp.sin(ang)
    lane = lax.broadcasted_iota(jnp.int32, ang.shape, 1)
    half = ROT_DIM // 2
    lo = lane < half
    hi = (lane >= ROT_PARTNER) & (lane < ROT_PARTNER + half)
    c_ref[...] = jnp.where(lo | hi, cos, 1.0)
    s_ref[...] = jnp.where(lo, -sin, jnp.where(hi, sin, 0.0))


def _rotary_tables(positions):
    tokens = positions.size
    bm = min(BM, tokens)
    pos = positions.reshape(tokens, 1).astype(F32)
    inv_freq = ROPE_THETA ** (-jnp.arange(0, ROT_DIM, 2, dtype=F32) / ROT_DIM)
    half = ROT_DIM // 2
    inv_lane = (jnp.zeros((1, HEAD_DIM), F32).at[0, :half].set(inv_freq)
                .at[0, ROT_PARTNER:ROT_PARTNER + half].set(inv_freq))
    spec = pl.BlockSpec((bm, HEAD_DIM), lambda i: (i, 0))
    out = jax.ShapeDtypeStruct((tokens, HEAD_DIM), F32)
    return pl.pallas_call(
        _rotary_table_kernel,
        grid=(tokens // bm,),
        in_specs=[pl.BlockSpec((bm, 1), lambda i: (i, 0)),
                  pl.BlockSpec((1, HEAD_DIM), lambda i: (0, 0))],
        out_specs=[spec, spec],
        out_shape=[out, out],
        compiler_params=_params("arbitrary"),
        name="rotary_tables",
    )(pos, inv_lane)


def _norm_proj_kernel(x_ref, g_ref, w_ref, o_ref, xn_ref, *, out_scale):
    @pl.when(pl.program_id(1) == 0)
    def _():
        xn_ref[...] = _rms_rows(x_ref[...], g_ref[...]).astype(BF16)

    acc = jnp.dot(xn_ref[...], w_ref[...], preferred_element_type=F32)
    if out_scale != 1.0:
        acc = acc * out_scale
    o_ref[...] = acc.astype(o_ref.dtype)


def _norm_proj(x, gain, w, *, out_scale=1.0, name):
    tokens, d = x.shape
    n = w.shape[1]
    bm, bn = min(BM, tokens), min(BN, n)
    assert tokens % bm == 0 and n % bn == 0, (tokens, bm, n, bn)
    return pl.pallas_call(
        functools.partial(_norm_proj_kernel, out_scale=out_scale),
        grid=(tokens // bm, n // bn),
        in_specs=[pl.BlockSpec((bm, d), lambda i, j: (i, 0)),
                  pl.BlockSpec((1, d), lambda i, j: (0, 0)),
                  pl.BlockSpec((d, bn), lambda i, j: (0, j))],
        out_specs=pl.BlockSpec((bm, bn), lambda i, j: (i, j)),
        out_shape=jax.ShapeDtypeStruct((tokens, n), BF16),
        scratch_shapes=[pltpu.VMEM((bm, d), BF16)],
        compiler_params=_params("arbitrary", "arbitrary"),
        name=name,
    )(x, gain.reshape(1, d), w)


def _qkv_kernel(x_ref, g_ref, w_ref, hg_ref, c_ref, s_ref, o_ref, xn_ref, *, n_qk_blocks):
    j = pl.program_id(1)

    @pl.when(j == 0)
    def _():
        xn_ref[...] = _rms_rows(x_ref[...], g_ref[...]).astype(BF16)

    acc = jnp.dot(xn_ref[...], w_ref[...], preferred_element_type=F32)
    bn = acc.shape[1]

    @pl.when(j < n_qk_blocks)
    def _():
        c, sn = c_ref[...], s_ref[...]
        for h in range(bn // HEAD_DIM):
            cols = slice(h * HEAD_DIM, (h + 1) * HEAD_DIM)
            y = _rms_rows(acc[:, cols], hg_ref[:, cols])
            rot = y * c + pltpu.roll(y, ROT_PARTNER, 1) * sn
            o_ref[:, cols] = rot.astype(o_ref.dtype)

    @pl.when(j >= n_qk_blocks)
    def _():
        o_ref[...] = acc.astype(o_ref.dtype)


def _qkv_proj(x, gain, w, head_gain, tables, *, qk_width, name):
    tokens, d = x.shape
    n = w.shape[1]
    bm, bn = min(BM, tokens), min(BN, qk_width)
    assert tokens % bm == 0 and qk_width % bn == 0 and n % bn == 0, (tokens, bm, qk_width, n, bn)
    tab = pl.BlockSpec((bm, HEAD_DIM), lambda i, j: (i, 0))
    return pl.pallas_call(
        functools.partial(_qkv_kernel, n_qk_blocks=2 * qk_width // bn),
        grid=(tokens // bm, n // bn),
        in_specs=[pl.BlockSpec((bm, d), lambda i, j: (i, 0)),
                  pl.BlockSpec((1, d), lambda i, j: (0, 0)),
                  pl.BlockSpec((d, bn), lambda i, j: (0, j)),
                  pl.BlockSpec((1, bn), lambda i, j: (0, j)),
                  tab, tab],
        out_specs=pl.BlockSpec((bm, bn), lambda i, j: (i, j)),
        out_shape=jax.ShapeDtypeStruct((tokens, n), BF16),
        scratch_shapes=[pltpu.VMEM((bm, d), BF16)],
        compiler_params=_params("arbitrary", "arbitrary"),
        name=name,
    )(x, gain.reshape(1, d), w, head_gain, *tables)


def _proj_residual_kernel(x_ref, w_ref, r_ref, o_ref):
    o_ref[...] = r_ref[...] + jnp.dot(x_ref[...], w_ref[...], preferred_element_type=F32)


def _proj_residual(x, w, res, *, bn, name):
    tokens, k = x.shape
    n = w.shape[1]
    bm, bn = min(BM, tokens), min(bn, n)
    assert tokens % bm == 0 and n % bn == 0, (tokens, bm, n, bn)
    return pl.pallas_call(
        _proj_residual_kernel,
        grid=(tokens // bm, n // bn),
        in_specs=[pl.BlockSpec((bm, k), lambda i, j: (i, 0)),
                  pl.BlockSpec((k, bn), lambda i, j: (0, j)),
                  pl.BlockSpec((bm, bn), lambda i, j: (i, j))],
        out_specs=pl.BlockSpec((bm, bn), lambda i, j: (i, j)),
        out_shape=jax.ShapeDtypeStruct((tokens, n), F32),
        compiler_params=_params("arbitrary", "arbitrary"),
        name=name,
    )(x, w, res)


def _ffn_up_kernel(x_ref, g_ref, wu_ref, wg_ref, cw_ref, cb_ref, o_ref, xn_ref, halo_ref, gbuf_ref,
                   *, blocks_per_seq):
    i, j = pl.program_id(0), pl.program_id(1)

    @pl.when(j == 0)
    def _():
        xn_ref[...] = _rms_rows(x_ref[...], g_ref[...]).astype(BF16)

    xn = xn_ref[...]
    u = jnp.dot(xn, wu_ref[...], preferred_element_type=F32)
    g = jnp.dot(xn, wg_ref[...], preferred_element_type=F32)
    bm = g.shape[0]

    gbuf_ref[0:SUBLANES, :] = jnp.where(i % blocks_per_seq == 0, 0.0, halo_ref[j])
    gbuf_ref[SUBLANES:, :] = g
    halo_ref[j] = g[bm - SUBLANES:, :]
    g1 = gbuf_ref[SUBLANES - 1:SUBLANES - 1 + bm, :]
    g2 = gbuf_ref[SUBLANES - 2:SUBLANES - 2 + bm, :]
    hc = cw_ref[2:3, :] * g + cw_ref[1:2, :] * g1 + cw_ref[0:1, :] * g2 + cb_ref[...]
    act = (hc * u) * (1.0 + jnp.tanh(hc))
    o_ref[...] = act.astype(o_ref.dtype)


def _ffn_up(x, gain, w_up, conv_w, conv_b, *, seq, name):
    tokens, d = x.shape
    d_ff = w_up.shape[1] // 2
    bm = min(BM, seq)
    bn = min(BN_FFN, d_ff)
    assert tokens % bm == 0 and d_ff % bn == 0, (tokens, bm, d_ff, bn)
    nj = d_ff // bn
    return pl.pallas_call(
        functools.partial(_ffn_up_kernel, blocks_per_seq=seq // bm),
        grid=(tokens // bm, nj),
        in_specs=[pl.BlockSpec((bm, d), lambda i, j: (i, 0)),
                  pl.BlockSpec((1, d), lambda i, j: (0, 0)),
                  pl.BlockSpec((d, bn), lambda i, j: (0, j)),
                  pl.BlockSpec((d, bn), lambda i, j: (0, j + nj)),
                  pl.BlockSpec((CONV_WIDTH, bn), lambda i, j: (0, j)),
                  pl.BlockSpec((1, bn), lambda i, j: (0, j))],
        out_specs=pl.BlockSpec((bm, bn), lambda i, j: (i, j)),
        out_shape=jax.ShapeDtypeStruct((tokens, d_ff), BF16),
        scratch_shapes=[pltpu.VMEM((bm, d), BF16),
                        pltpu.VMEM((nj, SUBLANES, bn), F32),
                        pltpu.VMEM((bm + SUBLANES, bn), F32)],
        compiler_params=_params("arbitrary", "arbitrary"),
        name=name,
    )(x, gain.reshape(1, d), w_up, w_up, 0.5 * conv_w, 0.5 * conv_b.reshape(1, d_ff))


def _all_sublanes(x, op):
    for d in (1, 2, 4):
        x = op(x, pltpu.roll(x, d, 0))
    return x


def _transposed_tiles(v3):
    b, s, w = v3.shape
    return v3.reshape(b, s // TK, TK, w).swapaxes(2, 3)


def _diff_attn_kernel(lq1_ref, lk1_ref, lq2_ref, lk2_ref, sg_ref, q_ref, k_ref, vt_ref, o_ref,
                      acc_ref, qt_ref, sc_ref, pc_ref, ac_ref, *, lambda_init, heads_per_step, lookahead):
    i = pl.program_id(2)
    tq = q_ref.shape[0]
    nv = TK // SUBLANES
    width = 2 * HEAD_DIM
    units = 2 * heads_per_step

    row = lax.broadcasted_iota(jnp.int32, (TK, tq), 0)
    col = lax.broadcasted_iota(jnp.int32, (TK, tq), 1)
    chunk_mask = (row // CHUNK) <= (col // CHUNK)

    acc_ref[...] = jnp.zeros_like(acc_ref)
    for u in range(units):
        qt_ref[u] = q_ref[:, u * HEAD_DIM:(u + 1) * HEAD_DIM].T

    def scores(kt, u):
        start = pl.multiple_of(kt * TK, TK)
        k = k_ref[pl.ds(start, TK), u * HEAD_DIM:(u + 1) * HEAD_DIM]
        return jnp.dot(k, qt_ref[u], preferred_element_type=F32)

    def softmax_step(s, m, lpart, masked):
        if masked:
            s = jnp.where(chunk_mask, s, -jnp.inf)
        p_cols, m_cols, l_cols, a_cols = [], [], [], []
        for ch in range(tq // LANES):
            lanes = slice(ch * LANES, (ch + 1) * LANES)
            s3 = s[:, lanes].reshape(nv, SUBLANES, LANES)
            tmax = s3[0]
            for v in range(1, nv):
                tmax = jnp.maximum(tmax, s3[v])
            m_new = jnp.maximum(m[:, lanes], _all_sublanes(tmax, jnp.maximum))
            alpha = jnp.exp2(m[:, lanes] - m_new)
            p3 = [jnp.exp2(s3[v] - m_new) for v in range(nv)]
            psum = p3[0]
            for v in range(1, nv):
                psum = psum + p3[v]
            p_cols.append(jnp.concatenate(p3, axis=0))
            m_cols.append(m_new)
            l_cols.append(alpha * lpart[:, lanes] + psum)
            a_cols.append(alpha)
        cat = lambda xs: jnp.concatenate(xs, axis=1)
        return cat(p_cols).astype(BF16), cat(m_cols), cat(l_cols), cat(a_cols)

    def rows(x):
        return jnp.concatenate([x] * (width // SUBLANES), axis=0)

    def accumulate(u, kt, alpha, p):
        hd = u // 2
        acc_ref[u] = rows(alpha) * acc_ref[u] + jnp.dot(vt_ref[kt, hd * width:(hd + 1) * width, :], p,
                                                        preferred_element_type=F32)

    def tile(kt, kt_next, carry, masked, first):
        if not first:
            accumulate(units - 1, kt + 1, ac_ref[...], pc_ref[...])
        new = []
        ahead = [sc_ref[d] for d in range(lookahead)]
        for u in range(units):
            s = ahead.pop(0)
            ahead.append(scores(kt, u + lookahead) if u + lookahead < units
                         else scores(kt_next, u + lookahead - units))
            p, m, l, a = softmax_step(s, carry[2 * u], carry[2 * u + 1], masked)
            new += [m, l]
            if u + 1 < units:
                accumulate(u, kt, a, p)
            else:
                pc_ref[...] = p
                ac_ref[...] = a
        for d in range(lookahead):
            sc_ref[d] = ahead[d]
        return tuple(new)

    for d in range(lookahead):
        sc_ref[d] = scores(i, d)
    neg = jnp.full((SUBLANES, tq), -jnp.inf, F32)
    zero = jnp.zeros((SUBLANES, tq), F32)
    carry = tile(i, jnp.maximum(i - 1, 0), (neg, zero) * units, True, True)
    carry = lax.fori_loop(
        0, i, lambda s, cr: tile(i - 1 - s, jnp.maximum(i - 2 - s, 0), cr, False, False), carry)
    accumulate(units - 1, 0, ac_ref[...], pc_ref[...])

    lam = (jnp.exp(jnp.sum(lq1_ref[...] * lk1_ref[...], axis=1, keepdims=True))
           - jnp.exp(jnp.sum(lq2_ref[...] * lk2_ref[...], axis=1, keepdims=True)) + lambda_init)
    for hd in range(heads_per_step):
        inv0 = 1.0 / _all_sublanes(carry[4 * hd + 1], jnp.add)
        inv1 = 1.0 / _all_sublanes(carry[4 * hd + 3], jnp.add)
        o_t = acc_ref[2 * hd] * rows(inv0) - lam * (acc_ref[2 * hd + 1] * rows(inv1))
        o = _rms_rows(o_t.T, sg_ref[...]) * (1.0 - lambda_init)
        o_ref[:, hd * width:(hd + 1) * width] = o.astype(o_ref.dtype)


def _diff_attention(qkv, lq1, lk1, lq2, lk2, subln_g, *, batch, seq, heads, lambda_init, name):
    width = 2 * HEAD_DIM
    hp = min(DIFF_HEADS_PER_STEP, heads)
    groups = heads // hp
    qkv3 = qkv.reshape(batch, seq, 3 * heads * width)
    vt = _transposed_tiles(qkv3[..., 2 * heads * width:])
    vec = pl.BlockSpec((1, HEAD_DIM), lambda b, h, i: (0, 0))
    out = pl.pallas_call(
        functools.partial(_diff_attn_kernel, lambda_init=lambda_init, heads_per_step=hp,
                          lookahead=min(LOOKAHEAD, 2 * hp)),
        grid=(batch, groups, seq // TQ),
        in_specs=[vec, vec, vec, vec,
                  pl.BlockSpec((1, width), lambda b, h, i: (0, 0)),
                  pl.BlockSpec((None, TQ, hp * width), lambda b, h, i: (b, i, h)),
                  pl.BlockSpec((None, seq, hp * width), lambda b, h, i: (b, 0, groups + h)),
                  pl.BlockSpec((None, seq // TK, hp * width, TK), lambda b, h, i: (b, 0, h, 0))],
        out_specs=pl.BlockSpec((None, TQ, hp * width), lambda b, h, i: (b, i, h)),
        out_shape=jax.ShapeDtypeStruct((batch, seq, heads * width), BF16),
        scratch_shapes=[pltpu.VMEM((2 * hp, width, TQ), F32),
                        pltpu.VMEM((2 * hp, HEAD_DIM, TQ), BF16),
                        pltpu.VMEM((LOOKAHEAD, TK, TQ), F32),
                        pltpu.VMEM((TK, TQ), BF16),
                        pltpu.VMEM((SUBLANES, TQ), F32)],
        compiler_params=_params("arbitrary", "arbitrary", "arbitrary"),
        name=name,
    )(lq1.reshape(1, -1), lk1.reshape(1, -1), lq2.reshape(1, -1), lk2.reshape(1, -1),
      subln_g.reshape(1, -1), qkv3, qkv3, vt)
    return out.reshape(batch * seq, heads * width)


def _stick_kernel(q_ref, k_ref, vt_ref, o_ref, acc_ref, qt_ref, zc_ref, ac_ref, *, heads_per_step, lookahead):
    i = pl.program_id(2)
    tq = q_ref.shape[0]
    nv = TK // SUBLANES
    hp = heads_per_step
    last = slice((hp - 1) * HEAD_DIM, hp * HEAD_DIM)

    row = lax.broadcasted_iota(jnp.int32, (TK, tq), 0)
    col = lax.broadcasted_iota(jnp.int32, (TK, tq), 1)
    causal = ((row % SUBLANES) * nv + row // SUBLANES) < col
    sub = lax.broadcasted_iota(jnp.int32, (SUBLANES, LANES), 0)

    acc_ref[...] = jnp.zeros_like(acc_ref)
    for hh in range(hp):
        qt_ref[hh] = q_ref[:, hh * HEAD_DIM:(hh + 1) * HEAD_DIM].T

    def scores(kt, hh):
        start = pl.multiple_of(kt * TK, TK)
        k = k_ref[pl.ds(start, TK), hh * HEAD_DIM:(hh + 1) * HEAD_DIM]
        return jnp.dot(k, qt_ref[hh], preferred_element_type=F32)

    def weights(zh, right, masked):
        beta = 0.5 * jnp.tanh(zh) + 0.5
        if masked:
            beta = jnp.where(causal, beta, 0.0)
        a_cols, totals = [], []
        for ch in range(tq // LANES):
            lanes = slice(ch * LANES, (ch + 1) * LANES)
            beta3 = beta[:, lanes].reshape(nv, SUBLANES, LANES)
            run = jnp.ones((SUBLANES, LANES), F32)
            part = [None] * nv
            for v in range(nv - 1, -1, -1):
                part[v] = beta3[v] * run
                run = run - part[v]
            t = run
            for d in (1, 2, 4):
                shifted = pltpu.roll(t, SUBLANES - d, 0)
                t = t * jnp.where(sub + d < SUBLANES, shifted, 1.0)
            later = jnp.where(sub + 1 < SUBLANES, pltpu.roll(t, SUBLANES - 1, 0), 1.0)
            off = right[:, lanes] * later
            a_cols.append(jnp.concatenate([part[v] * off for v in range(nv)], axis=0))
            totals.append(jnp.broadcast_to(t[0:1, :], (SUBLANES, LANES)))
        a = jnp.concatenate(a_cols, axis=1).astype(BF16)
        return a, right * jnp.concatenate(totals, axis=1)

    def tile(kt, kt_next, right, masked, first):
        if not first:
            acc_ref[hp - 1] += jnp.dot(vt_ref[kt + 1, last, :], ac_ref[...], preferred_element_type=F32)
        new_right = []
        ahead = [zc_ref[d] for d in range(lookahead)]
        for hh in range(hp):
            z = ahead.pop(0)
            ahead.append(scores(kt, hh + lookahead) if hh + lookahead < hp
                         else scores(kt_next, hh + lookahead - hp))
            a, r = weights(z, right[hh], masked)
            new_right.append(r)
            if hh + 1 < hp:
                acc_ref[hh] += jnp.dot(vt_ref[kt, hh * HEAD_DIM:(hh + 1) * HEAD_DIM, :], a,
                                       preferred_element_type=F32)
            else:
                ac_ref[...] = a
        for d in range(lookahead):
            zc_ref[d] = ahead[d]
        return tuple(new_right)

    for d in range(lookahead):
        zc_ref[d] = scores(i, d)
    ones = tuple(jnp.ones((SUBLANES, tq), F32) for _ in range(hp))
    right = tile(i, jnp.maximum(i - 1, 0), ones, True, True)
    lax.fori_loop(0, i, lambda s, rt: tile(i - 1 - s, jnp.maximum(i - 2 - s, 0), rt, False, False), right)
    acc_ref[hp - 1] += jnp.dot(vt_ref[0, last, :], ac_ref[...], preferred_element_type=F32)
    for hh in range(hp):
        o_ref[:, hh * HEAD_DIM:(hh + 1) * HEAD_DIM] = acc_ref[hh].T.astype(o_ref.dtype)


def _permute_keys(k3, v3):
    b, s, w = k3.shape
    nv = TK // SUBLANES
    kp = k3.reshape(b, s // TK, SUBLANES, nv, w).swapaxes(2, 3).reshape(b, s, w)
    vt = v3.reshape(b, s // TK, SUBLANES, nv, w).transpose(0, 1, 4, 3, 2).reshape(b, s // TK, w, TK)
    return kp, vt


def _stick_breaking(q, kp, vt, *, batch, seq, heads, name):
    width = heads * HEAD_DIM
    hp = min(STICK_HEADS_PER_STEP, heads)
    q3 = q.reshape(batch, seq, width)
    out = pl.pallas_call(
        functools.partial(_stick_kernel, heads_per_step=hp, lookahead=min(LOOKAHEAD, hp)),
        grid=(batch, heads // hp, seq // TQ),
        in_specs=[pl.BlockSpec((None, TQ, hp * HEAD_DIM), lambda b, h, i: (b, i, h)),
                  pl.BlockSpec((None, seq, hp * HEAD_DIM), lambda b, h, i: (b, 0, h)),
                  pl.BlockSpec((None, seq // TK, hp * HEAD_DIM, TK), lambda b, h, i: (b, 0, h, 0))],
        out_specs=pl.BlockSpec((None, TQ, hp * HEAD_DIM), lambda b, h, i: (b, i, h)),
        out_shape=jax.ShapeDtypeStruct((batch, seq, width), BF16),
        scratch_shapes=[pltpu.VMEM((hp, HEAD_DIM, TQ), F32),
                        pltpu.VMEM((hp, HEAD_DIM, TQ), BF16),
                        pltpu.VMEM((LOOKAHEAD, TK, TQ), F32),
                        pltpu.VMEM((TK, TQ), BF16)],
        compiler_params=_params("arbitrary", "arbitrary", "arbitrary"),
        name=name,
    )(q3, kp, vt)
    return out.reshape(batch * seq, width)


def kernel(x, positions, attn_norm_g, ffn_norm_g, a_w_qkv, a_q_norm_g, a_k_norm_g, a_lambda_q1, a_lambda_k1, a_lambda_q2, a_lambda_k2, a_subln_g, a_w_o, kv_norm_g, b_w_kv, b_w_q, b_w_o, ffn_w_up, ffn_conv_w, ffn_conv_b, ffn_w_down):
    batch, seq, d_model = x.shape
    depth = attn_norm_g.shape[0]
    n_a = a_w_qkv.shape[0]
    a_heads = d_model // (2 * HEAD_DIM)
    b_heads = d_model // HEAD_DIM
    qk_width = a_heads * 2 * HEAD_DIM
    scale = HEAD_DIM ** -0.5

    h = x.reshape(batch * seq, d_model)
    tables = _rotary_tables(positions)
    order = _head_dim_order()
    kp = vt = None
    for layer in range(depth):
        if layer < n_a:
            lambda_init = 0.8 - 0.6 * math.exp(-0.3 * layer)
            n_qk_heads = 2 * qk_width // HEAD_DIM
            col_order = jnp.concatenate([
                (jnp.arange(n_qk_heads)[:, None] * HEAD_DIM + order[None, :]).reshape(-1),
                jnp.arange(2 * qk_width, a_w_qkv.shape[2])])
            head_gain = jnp.concatenate([
                jnp.tile(a_q_norm_g[layer][order] * (scale * math.log2(math.e)), qk_width // HEAD_DIM),
                jnp.tile(a_k_norm_g[layer][order], qk_width // HEAD_DIM),
                jnp.ones((a_w_qkv.shape[2] - 2 * qk_width,), F32)]).reshape(1, -1)
            qkv = _qkv_proj(h, attn_norm_g[layer], a_w_qkv[layer][:, col_order].astype(BF16), head_gain,
                            tables, qk_width=qk_width, name=f"a{layer}_qkv")
            o = _diff_attention(qkv, a_lambda_q1[layer], a_lambda_k1[layer], a_lambda_q2[layer],
                                a_lambda_k2[layer], a_subln_g[layer], batch=batch, seq=seq,
                                heads=a_heads, lambda_init=lambda_init, name=f"a{layer}_attn")
            h = _proj_residual(o, a_w_o[layer].astype(BF16), h, bn=BN, name=f"a{layer}_out")
        else:
            if layer == n_a:
                kv = _norm_proj(h, kv_norm_g, b_w_kv.astype(BF16), name="b_kv")
                kv = kv.reshape(batch, seq, 2 * b_heads * HEAD_DIM)
                kp, vt = _permute_keys(kv[..., :b_heads * HEAD_DIM], kv[..., b_heads * HEAD_DIM:])
            j = layer - n_a
            q = _norm_proj(h, attn_norm_g[layer], b_w_q[j].astype(BF16), out_scale=0.5 * scale,
                           name=f"b{j}_q")
            o = _stick_breaking(q, kp, vt, batch=batch, seq=seq, heads=b_heads, name=f"b{j}_attn")
            h = _proj_residual(o, b_w_o[j].astype(BF16), h, bn=BN, name=f"b{j}_out")
        act = _ffn_up(h, ffn_norm_g[layer], ffn_w_up[layer].astype(BF16), ffn_conv_w[layer],
                      ffn_conv_b[layer], seq=seq, name=f"l{layer}_ffn_up")
        h = _proj_residual(act, ffn_w_down[layer].astype(BF16), h, bn=BN_DOWN, name=f"l{layer}_ffn_down")
    return h.reshape(batch, seq, d_model)
```

```python
import functools
import math

import jax
import jax.numpy as jnp
from jax import lax
from jax.experimental import pallas as pl
from jax.experimental.pallas import tpu as pltpu

F32 = jnp.float32
BF16 = jnp.bfloat16

CHUNK = 64
HEAD_DIM = 128
ROT_DIM = HEAD_DIM // 4
ROT_PARTNER = HEAD_DIM // 2
ROPE_THETA = 500000.0
CONV_WIDTH = 3
EPS = 1e-6

LANES = 128
SUBLANES = 8
VMEM_LIMIT_BYTES = 56 * 1024 * 1024

BM = 1024
BN = 1024
BN_FFN = 512
BN_DOWN = 512
TQ = 256
TK = 256
DIFF_HEADS_PER_STEP = 8
STICK_HEADS_PER_STEP = 16
LOOKAHEAD = 3


def _params(*sem):
    return pltpu.CompilerParams(dimension_semantics=sem, vmem_limit_bytes=VMEM_LIMIT_BYTES)


def _rms_rows(x, gain):
    ms = jnp.mean(x * x, axis=-1, keepdims=True)
    return x * lax.rsqrt(ms + EPS) * gain


def _head_dim_order():
    half = ROT_DIM // 2
    return jnp.concatenate([jnp.arange(0, half), jnp.arange(ROT_DIM, ROT_PARTNER + half),
                            jnp.arange(half, ROT_DIM), jnp.arange(ROT_PARTNER + half, HEAD_DIM)])


def _rotary_table_kernel(pos_ref, inv_ref, c_ref, s_ref):
    ang = pos_ref[...] * inv_ref[...]
    cos = jnp.cos(ang)
    sin = jnp.sin(ang)
    lane = lax.broadcasted_iota(jnp.int32, ang.shape, 1)
    half = ROT_DIM // 2
    lo = lane < half
    hi = (lane >= ROT_PARTNER) & (lane < ROT_PARTNER + half)
    c_ref[...] = jnp.where(lo | hi, cos, 1.0)
    s_ref[...] = jnp.where(lo, -sin, jnp.where(hi, sin, 0.0))


def _rotary_tables(positions):
    tokens = positions.size
    bm = min(BM, tokens)
    pos = positions.reshape(tokens, 1).astype(F32)
    inv_freq = ROPE_THETA ** (-jnp.arange(0, ROT_DIM, 2, dtype=F32) / ROT_DIM)
    half = ROT_DIM // 2
    inv_lane = (jnp.zeros((1, HEAD_DIM), F32).at[0, :half].set(inv_freq)
                .at[0, ROT_PARTNER:ROT_PARTNER + half].set(inv_freq))
    spec = pl.BlockSpec((bm, HEAD_DIM), lambda i: (i, 0))
    out = jax.ShapeDtypeStruct((tokens, HEAD_DIM), F32)
    return pl.pallas_call(
        _rotary_table_kernel,
        grid=(tokens // bm,),
        in_specs=[pl.BlockSpec((bm, 1), lambda i: (i, 0)),
                  pl.BlockSpec((1, HEAD_DIM), lambda i: (0, 0))],
        out_specs=[spec, spec],
        out_shape=[out, out],
        compiler_params=_params("arbitrary"),
        name="rotary_tables",
    )(pos, inv_lane)


def _norm_proj_kernel(x_ref, g_ref, w_ref, o_ref, xn_ref, *, out_scale):
    @pl.when(pl.program_id(1) == 0)
    def _():
        xn_ref[...] = _rms_rows(x_ref[...], g_ref[...]).astype(BF16)

    acc = jnp.dot(xn_ref[...], w_ref[...], preferred_element_type=F32)
    if out_scale != 1.0:
        acc = acc * out_scale
    o_ref[...] = acc.astype(o_ref.dtype)


def _norm_proj(x, gain, w, *, out_scale=1.0, name):
    tokens, d = x.shape
    n = w.shape[1]
    bm, bn = min(BM, tokens), min(BN, n)
    assert tokens % bm == 0 and n % bn == 0, (tokens, bm, n, bn)
    return pl.pallas_call(
        functools.partial(_norm_proj_kernel, out_scale=out_scale),
        grid=(tokens // bm, n // bn),
        in_specs=[pl.BlockSpec((bm, d), lambda i, j: (i, 0)),
                  pl.BlockSpec((1, d), lambda i, j: (0, 0)),
                  pl.BlockSpec((d, bn), lambda i, j: (0, j))],
        out_specs=pl.BlockSpec((bm, bn), lambda i, j: (i, j)),
        out_shape=jax.ShapeDtypeStruct((tokens, n), BF16),
        scratch_shapes=[pltpu.VMEM((bm, d), BF16)],
        compiler_params=_params("arbitrary", "arbitrary"),
        name=name,
    )(x, gain.reshape(1, d), w)


def _qkv_kernel(x_ref, g_ref, w_ref, wvt_ref, hg_ref, c_ref, s_ref, qk_ref, vt_ref, xn_ref, *, n_qk_blocks):
    j = pl.program_id(1)

    @pl.when(j == 0)
    def _():
        xn_ref[...] = _rms_rows(x_ref[...], g_ref[...]).astype(BF16)

    @pl.when(j < n_qk_blocks)
    def _():
        acc = jnp.dot(xn_ref[...], w_ref[...], preferred_element_type=F32)
        c, sn = c_ref[...], s_ref[...]
        for h in range(acc.shape[1] // HEAD_DIM):
            cols = slice(h * HEAD_DIM, (h + 1) * HEAD_DIM)
            y = _rms_rows(acc[:, cols], hg_ref[:, cols])
            rot = y * c + pltpu.roll(y, ROT_PARTNER, 1) * sn
            qk_ref[:, cols] = rot.astype(qk_ref.dtype)

    @pl.when(j >= n_qk_blocks)
    def _():
        acc = lax.dot_general(wvt_ref[...], xn_ref[...], (((1,), (1,)), ((), ())),
                              preferred_element_type=F32)
        for t in range(vt_ref.shape[0]):
            vt_ref[t] = acc[:, t * TK:(t + 1) * TK].astype(vt_ref.dtype)


def _qkv_proj(x, gain, w_qk, w_vt, head_gain, tables, *, name):
    tokens, d = x.shape
    n_qk, n_v = w_qk.shape[1], w_vt.shape[0]
    bm, bn = min(BM, tokens), min(BN, n_v)
    assert tokens % bm == 0 and n_qk % bn == 0 and n_v % bn == 0 and bm % TK == 0, (tokens, bm, n_qk, n_v, bn)
    nqk = n_qk // bn
    tab = pl.BlockSpec((bm, HEAD_DIM), lambda i, j: (i, 0))
    qk_col = lambda i, j: (0, jnp.minimum(j, nqk - 1))
    return pl.pallas_call(
        functools.partial(_qkv_kernel, n_qk_blocks=nqk),
        grid=(tokens // bm, nqk + n_v // bn),
        in_specs=[pl.BlockSpec((bm, d), lambda i, j: (i, 0)),
                  pl.BlockSpec((1, d), lambda i, j: (0, 0)),
                  pl.BlockSpec((d, bn), qk_col),
                  pl.BlockSpec((bn, d), lambda i, j: (jnp.maximum(j - nqk, 0), 0)),
                  pl.BlockSpec((1, bn), qk_col),
                  tab, tab],
        out_specs=[pl.BlockSpec((bm, bn), lambda i, j: (i, jnp.minimum(j, nqk - 1))),
                   pl.BlockSpec((bm // TK, bn, TK), lambda i, j: (i, jnp.maximum(j - nqk, 0), 0))],
        out_shape=[jax.ShapeDtypeStruct((tokens, n_qk), BF16),
                   jax.ShapeDtypeStruct((tokens // TK, n_v, TK), BF16)],
        scratch_shapes=[pltpu.VMEM((bm, d), BF16)],
        compiler_params=_params("arbitrary", "arbitrary"),
        name=name,
    )(x, gain.reshape(1, d), w_qk, w_vt, head_gain, *tables)


def _proj_residual_kernel(x_ref, w_ref, r_ref, o_ref):
    o_ref[...] = r_ref[...] + jnp.dot(x_ref[...], w_ref[...], preferred_element_type=F32)


def _proj_residual(x, w, res, *, bn, name):
    tokens, k = x.shape
    n = w.shape[1]
    bm, bn = min(BM, tokens), min(bn, n)
    assert tokens % bm == 0 and n % bn == 0, (tokens, bm, n, bn)
    return pl.pallas_call(
        _proj_residual_kernel,
        grid=(tokens // bm, n // bn),
        in_specs=[pl.BlockSpec((bm, k), lambda i, j: (i, 0)),
                  pl.BlockSpec((k, bn), lambda i, j: (0, j)),
                  pl.BlockSpec((bm, bn), lambda i, j: (i, j))],
        out_specs=pl.BlockSpec((bm, bn), lambda i, j: (i, j)),
        out_shape=jax.ShapeDtypeStruct((tokens, n), F32),
        compiler_params=_params("arbitrary", "arbitrary"),
        name=name,
    )(x, w, res)


def _ffn_up_kernel(x_ref, g_ref, wu_ref, wg_ref, cw_ref, cb_ref, o_ref, xn_ref, halo_ref, gbuf_ref,
                   *, blocks_per_seq):
    i, j = pl.program_id(0), pl.program_id(1)

    @pl.when(j == 0)
    def _():
        xn_ref[...] = _rms_rows(x_ref[...], g_ref[...]).astype(BF16)

    xn = xn_ref[...]
    u = jnp.dot(xn, wu_ref[...], preferred_element_type=F32)
    g = jnp.dot(xn, wg_ref[...], preferred_element_type=F32)
    bm = g.shape[0]

    gbuf_ref[0:SUBLANES, :] = jnp.where(i % blocks_per_seq == 0, 0.0, halo_ref[j])
    gbuf_ref[SUBLANES:, :] = g
    halo_ref[j] = g[bm - SUBLANES:, :]
    g1 = gbuf_ref[SUBLANES - 1:SUBLANES - 1 + bm, :]
    g2 = gbuf_ref[SUBLANES - 2:SUBLANES - 2 + bm, :]
    hc = cw_ref[2:3, :] * g + cw_ref[1:2, :] * g1 + cw_ref[0:1, :] * g2 + cb_ref[...]
    act = (hc * u) * (1.0 + jnp.tanh(hc))
    o_ref[...] = act.astype(o_ref.dtype)


def _ffn_up(x, gain, w_up, conv_w, conv_b, *, seq, name):
    tokens, d = x.shape
    d_ff = w_up.shape[1] // 2
    bm = min(BM, seq)
    bn = min(BN_FFN, d_ff)
    assert tokens % bm == 0 and d_ff % bn == 0, (tokens, bm, d_ff, bn)
    nj = d_ff // bn
    return pl.pallas_call(
        functools.partial(_ffn_up_kernel, blocks_per_seq=seq // bm),
        grid=(tokens // bm, nj),
        in_specs=[pl.BlockSpec((bm, d), lambda i, j: (i, 0)),
                  pl.BlockSpec((1, d), lambda i, j: (0, 0)),
                  pl.BlockSpec((d, bn), lambda i, j: (0, j)),
                  pl.BlockSpec((d, bn), lambda i, j: (0, j + nj)),
                  pl.BlockSpec((CONV_WIDTH, bn), lambda i, j: (0, j)),
                  pl.BlockSpec((1, bn), lambda i, j: (0, j))],
        out_specs=pl.BlockSpec((bm, bn), lambda i, j: (i, j)),
        out_shape=jax.ShapeDtypeStruct((tokens, d_ff), BF16),
        scratch_shapes=[pltpu.VMEM((bm, d), BF16),
                        pltpu.VMEM((nj, SUBLANES, bn), F32),
                        pltpu.VMEM((bm + SUBLANES, bn), F32)],
        compiler_params=_params("arbitrary", "arbitrary"),
        name=name,
    )(x, gain.reshape(1, d), w_up, w_up, 0.5 * conv_w, 0.5 * conv_b.reshape(1, d_ff))


def _all_sublanes(x, op):
    for d in (1, 2, 4):
        x = op(x, pltpu.roll(x, d, 0))
    return x


def _diff_attn_kernel(lq1_ref, lk1_ref, lq2_ref, lk2_ref, sg_ref, q_ref, k_ref, vt_ref, o_ref,
                      acc_ref, qt_ref, sc_ref, pc_ref, ac_ref, *, lambda_init, heads_per_step, lookahead):
    i = pl.program_id(2)
    tq = q_ref.shape[0]
    nv = TK // SUBLANES
    width = 2 * HEAD_DIM
    units = 2 * heads_per_step

    row = lax.broadcasted_iota(jnp.int32, (TK, tq), 0)
    col = lax.broadcasted_iota(jnp.int32, (TK, tq), 1)
    chunk_mask = (row // CHUNK) <= (col // CHUNK)

    acc_ref[...] = jnp.zeros_like(acc_ref)
    for u in range(units):
        qt_ref[u] = q_ref[:, u * HEAD_DIM:(u + 1) * HEAD_DIM].T

    def scores(kt, u):
        start = pl.multiple_of(kt * TK, TK)
        k = k_ref[pl.ds(start, TK), u * HEAD_DIM:(u + 1) * HEAD_DIM]
        return jnp.dot(k, qt_ref[u], preferred_element_type=F32)

    def softmax_step(s, m, lpart, masked):
        if masked:
            s = jnp.where(chunk_mask, s, -jnp.inf)
        p_cols, m_cols, l_cols, a_cols = [], [], [], []
        for ch in range(tq // LANES):
            lanes = slice(ch * LANES, (ch + 1) * LANES)
            s3 = s[:, lanes].reshape(nv, SUBLANES, LANES)
            tmax = s3[0]
            for v in range(1, nv):
                tmax = jnp.maximum(tmax, s3[v])
            m_new = jnp.maximum(m[:, lanes], _all_sublanes(tmax, jnp.maximum))
            alpha = jnp.exp2(m[:, lanes] - m_new)
            p3 = [jnp.exp2(s3[v] - m_new) for v in range(nv)]
            psum = p3[0]
            for v in range(1, nv):
                psum = psum + p3[v]
            p_cols.append(jnp.concatenate(p3, axis=0))
            m_cols.append(m_new)
            l_cols.append(alpha * lpart[:, lanes] + psum)
            a_cols.append(alpha)
        cat = lambda xs: jnp.concatenate(xs, axis=1)
        return cat(p_cols).astype(BF16), cat(m_cols), cat(l_cols), cat(a_cols)

    def rows(x):
        return jnp.concatenate([x] * (width // SUBLANES), axis=0)

    def accumulate(u, kt, alpha, p):
        hd = u // 2
        acc_ref[u] = rows(alpha) * acc_ref[u] + jnp.dot(vt_ref[kt, hd * width:(hd + 1) * width, :], p,
                                                        preferred_element_type=F32)

    def tile(kt, kt_next, carry, masked, first):
        if not first:
            accumulate(units - 1, kt + 1, ac_ref[...], pc_ref[...])
        new = []
        ahead = [sc_ref[d] for d in range(lookahead)]
        for u in range(units):
            s = ahead.pop(0)
            ahead.append(scores(kt, u + lookahead) if u + lookahead < units
                         else scores(kt_next, u + lookahead - units))
            p, m, l, a = softmax_step(s, carry[2 * u], carry[2 * u + 1], masked)
            new += [m, l]
            if u + 1 < units:
                accumulate(u, kt, a, p)
            else:
                pc_ref[...] = p
                ac_ref[...] = a
        for d in range(lookahead):
            sc_ref[d] = ahead[d]
        return tuple(new)

    for d in range(lookahead):
        sc_ref[d] = scores(i, d)
    neg = jnp.full((SUBLANES, tq), -jnp.inf, F32)
    zero = jnp.zeros((SUBLANES, tq), F32)
    carry = tile(i, jnp.maximum(i - 1, 0), (neg, zero) * units, True, True)
    carry = lax.fori_loop(
        0, i, lambda s, cr: tile(i - 1 - s, jnp.maximum(i - 2 - s, 0), cr, False, False), carry)
    accumulate(units - 1, 0, ac_ref[...], pc_ref[...])

    lam = (jnp.exp(jnp.sum(lq1_ref[...] * lk1_ref[...], axis=1, keepdims=True))
           - jnp.exp(jnp.sum(lq2_ref[...] * lk2_ref[...], axis=1, keepdims=True)) + lambda_init)
    for hd in range(heads_per_step):
        inv0 = 1.0 / _all_sublanes(carry[4 * hd + 1], jnp.add)
        inv1 = 1.0 / _all_sublanes(carry[4 * hd + 3], jnp.add)
        o_t = acc_ref[2 * hd] * rows(inv0) - lam * (acc_ref[2 * hd + 1] * rows(inv1))
        o = _rms_rows(o_t.T, sg_ref[...]) * (1.0 - lambda_init)
        o_ref[:, hd * width:(hd + 1) * width] = o.astype(o_ref.dtype)


def _diff_attention(qk, vt, lq1, lk1, lq2, lk2, subln_g, *, batch, seq, heads, lambda_init, name):
    width = 2 * HEAD_DIM
    hp = min(DIFF_HEADS_PER_STEP, heads)
    groups = heads // hp
    qk3 = qk.reshape(batch, seq, 2 * heads * width)
    vt = vt.reshape(batch, seq // TK, heads * width, TK)
    vec = pl.BlockSpec((1, HEAD_DIM), lambda b, h, i: (0, 0))
    out = pl.pallas_call(
        functools.partial(_diff_attn_kernel, lambda_init=lambda_init, heads_per_step=hp,
                          lookahead=min(LOOKAHEAD, 2 * hp)),
        grid=(batch, groups, seq // TQ),
        in_specs=[vec, vec, vec, vec,
                  pl.BlockSpec((1, width), lambda b, h, i: (0, 0)),
                  pl.BlockSpec((None, TQ, hp * width), lambda b, h, i: (b, i, h)),
                  pl.BlockSpec((None, seq, hp * width), lambda b, h, i: (b, 0, groups + h)),
                  pl.BlockSpec((None, seq // TK, hp * width, TK), lambda b, h, i: (b, 0, h, 0))],
        out_specs=pl.BlockSpec((None, TQ, hp * width), lambda b, h, i: (b, i, h)),
        out_shape=jax.ShapeDtypeStruct((batch, seq, heads * width), BF16),
        scratch_shapes=[pltpu.VMEM((2 * hp, width, TQ), F32),
                        pltpu.VMEM((2 * hp, HEAD_DIM, TQ), BF16),
                        pltpu.VMEM((LOOKAHEAD, TK, TQ), F32),
                        pltpu.VMEM((TK, TQ), BF16),
                        pltpu.VMEM((SUBLANES, TQ), F32)],
        compiler_params=_params("arbitrary", "arbitrary", "arbitrary"),
        name=name,
    )(lq1.reshape(1, -1), lk1.reshape(1, -1), lq2.reshape(1, -1), lk2.reshape(1, -1),
      subln_g.reshape(1, -1), qk3, qk3, vt)
    return out.reshape(batch * seq, heads * width)


def _stick_kernel(q_ref, k_ref, vt_ref, o_ref, acc_ref, qt_ref, zc_ref, ac_ref, *, heads_per_step, lookahead):
    i = pl.program_id(2)
    tq = q_ref.shape[0]
    nv = TK // SUBLANES
    hp = heads_per_step
    last = slice((hp - 1) * HEAD_DIM, hp * HEAD_DIM)

    row = lax.broadcasted_iota(jnp.int32, (TK, tq), 0)
    col = lax.broadcasted_iota(jnp.int32, (TK, tq), 1)
    causal = ((row % SUBLANES) * nv + row // SUBLANES) < col
    sub = lax.broadcasted_iota(jnp.int32, (SUBLANES, LANES), 0)

    acc_ref[...] = jnp.zeros_like(acc_ref)
    for hh in range(hp):
        qt_ref[hh] = q_ref[:, hh * HEAD_DIM:(hh + 1) * HEAD_DIM].T

    def scores(kt, hh):
        start = pl.multiple_of(kt * TK, TK)
        k = k_ref[pl.ds(start, TK), hh * HEAD_DIM:(hh + 1) * HEAD_DIM]
        return jnp.dot(k, qt_ref[hh], preferred_element_type=F32)

    def weights(zh, right, masked):
        beta = 0.5 * jnp.tanh(zh) + 0.5
        if masked:
            beta = jnp.where(causal, beta, 0.0)
        a_cols, totals = [], []
        for ch in range(tq // LANES):
            lanes = slice(ch * LANES, (ch + 1) * LANES)
            beta3 = beta[:, lanes].reshape(nv, SUBLANES, LANES)
            run = jnp.ones((SUBLANES, LANES), F32)
            part = [None] * nv
            for v in range(nv - 1, -1, -1):
                part[v] = beta3[v] * run
                run = run - part[v]
            t = run
            for d in (1, 2, 4):
                shifted = pltpu.roll(t, SUBLANES - d, 0)
                t = t * jnp.where(sub + d < SUBLANES, shifted, 1.0)
            later = jnp.where(sub + 1 < SUBLANES, pltpu.roll(t, SUBLANES - 1, 0), 1.0)
            off = right[:, lanes] * later
            a_cols.append(jnp.concatenate([part[v] * off for v in range(nv)], axis=0))
            totals.append(jnp.broadcast_to(t[0:1, :], (SUBLANES, LANES)))
        a = jnp.concatenate(a_cols, axis=1).astype(BF16)
        return a, right * jnp.concatenate(totals, axis=1)

    def tile(kt, kt_next, right, masked, first):
        if not first:
            acc_ref[hp - 1] += jnp.dot(vt_ref[kt + 1, last, :], ac_ref[...], preferred_element_type=F32)
        new_right = []
        ahead = [zc_ref[d] for d in range(lookahead)]
        for hh in range(hp):
            z = ahead.pop(0)
            ahead.append(scores(kt, hh + lookahead) if hh + lookahead < hp
                         else scores(kt_next, hh + lookahead - hp))
            a, r = weights(z, right[hh], masked)
            new_right.append(r)
            if hh + 1 < hp:
                acc_ref[hh] += jnp.dot(vt_ref[kt, hh * HEAD_DIM:(hh + 1) * HEAD_DIM, :], a,
                                       preferred_element_type=F32)
            else:
                ac_ref[...] = a
        for d in range(lookahead):
            zc_ref[d] = ahead[d]
        return tuple(new_right)

    for d in range(lookahead):
        zc_ref[d] = scores(i, d)
    ones = tuple(jnp.ones((SUBLANES, tq), F32) for _ in range(hp))
    right = tile(i, jnp.maximum(i - 1, 0), ones, True, True)
    lax.fori_loop(0, i, lambda s, rt: tile(i - 1 - s, jnp.maximum(i - 2 - s, 0), rt, False, False), right)
    acc_ref[hp - 1] += jnp.dot(vt_ref[0, last, :], ac_ref[...], preferred_element_type=F32)
    for hh in range(hp):
        o_ref[:, hh * HEAD_DIM:(hh + 1) * HEAD_DIM] = acc_ref[hh].T.astype(o_ref.dtype)


def _stick_breaking(q, kp, vt, *, batch, seq, heads, name):
    width = heads * HEAD_DIM
    hp = min(STICK_HEADS_PER_STEP, heads)
    q3 = q.reshape(batch, seq, width)
    kp = kp.reshape(batch, seq, width)
    vt = vt.reshape(batch, seq // TK, width, TK)
    out = pl.pallas_call(
        functools.partial(_stick_kernel, heads_per_step=hp, lookahead=min(LOOKAHEAD, hp)),
        grid=(batch, heads // hp, seq // TQ),
        in_specs=[pl.BlockSpec((None, TQ, hp * HEAD_DIM), lambda b, h, i: (b, i, h)),
                  pl.BlockSpec((None, seq, hp * HEAD_DIM), lambda b, h, i: (b, 0, h)),
                  pl.BlockSpec((None, seq // TK, hp * HEAD_DIM, TK), lambda b, h, i: (b, 0, h, 0))],
        out_specs=pl.BlockSpec((None, TQ, hp * HEAD_DIM), lambda b, h, i: (b, i, h)),
        out_shape=jax.ShapeDtypeStruct((batch, seq, width), BF16),
        scratch_shapes=[pltpu.VMEM((hp, HEAD_DIM, TQ), F32),
                        pltpu.VMEM((hp, HEAD_DIM, TQ), BF16),
                        pltpu.VMEM((LOOKAHEAD, TK, TQ), F32),
                        pltpu.VMEM((TK, TQ), BF16)],
        compiler_params=_params("arbitrary", "arbitrary", "arbitrary"),
        name=name,
    )(q3, kp, vt)
    return out.reshape(batch * seq, width)


def _kv_kernel(x_ref, g_ref, wk_ref, wvt_ref, k_ref, vt_ref, xn_ref, *, n_k_blocks):
    j = pl.program_id(1)
    bm = x_ref.shape[0]
    nv = TK // SUBLANES

    @pl.when(j == 0)
    def _():
        xn = _rms_rows(x_ref[...], g_ref[...]).astype(BF16)
        r = lax.broadcasted_iota(jnp.int32, (TK, TK), 0)
        t = lax.broadcasted_iota(jnp.int32, (TK, TK), 1)
        pick = jnp.where(t == (r % SUBLANES) * nv + r // SUBLANES, 1.0, 0.0).astype(BF16)
        for grp in range(bm // TK):
            rows = slice(grp * TK, (grp + 1) * TK)
            xn_ref[rows, :] = jnp.dot(pick, xn[rows, :], preferred_element_type=F32).astype(BF16)

    @pl.when(j < n_k_blocks)
    def _():
        k_ref[...] = jnp.dot(xn_ref[...], wk_ref[...], preferred_element_type=F32).astype(k_ref.dtype)

    @pl.when(j >= n_k_blocks)
    def _():
        acc = lax.dot_general(wvt_ref[...], xn_ref[...], (((1,), (1,)), ((), ())),
                              preferred_element_type=F32)
        for t in range(vt_ref.shape[0]):
            vt_ref[t] = acc[:, t * TK:(t + 1) * TK].astype(vt_ref.dtype)


def _kv_proj(x, gain, wk, wvt, *, name):
    tokens, d = x.shape
    n = wk.shape[1]
    bm, bn = min(BM, tokens), min(BN, n)
    assert tokens % bm == 0 and n % bn == 0 and bm % TK == 0, (tokens, bm, n, bn)
    nk = n // bn
    return pl.pallas_call(
        functools.partial(_kv_kernel, n_k_blocks=nk),
        grid=(tokens // bm, 2 * nk),
        in_specs=[pl.BlockSpec((bm, d), lambda i, j: (i, 0)),
                  pl.BlockSpec((1, d), lambda i, j: (0, 0)),
                  pl.BlockSpec((d, bn), lambda i, j: (0, jnp.minimum(j, nk - 1))),
                  pl.BlockSpec((bn, d), lambda i, j: (jnp.maximum(j - nk, 0), 0))],
        out_specs=[pl.BlockSpec((bm, bn), lambda i, j: (i, jnp.minimum(j, nk - 1))),
                   pl.BlockSpec((bm // TK, bn, TK), lambda i, j: (i, jnp.maximum(j - nk, 0), 0))],
        out_shape=[jax.ShapeDtypeStruct((tokens, n), BF16),
                   jax.ShapeDtypeStruct((tokens // TK, n, TK), BF16)],
        scratch_shapes=[pltpu.VMEM((bm, d), BF16)],
        compiler_params=_params("arbitrary", "arbitrary"),
        name=name,
    )(x, gain.reshape(1, d), wk, wvt)


def kernel(x, positions, attn_norm_g, ffn_norm_g, a_w_qkv, a_q_norm_g, a_k_norm_g, a_lambda_q1, a_lambda_k1, a_lambda_q2, a_lambda_k2, a_subln_g, a_w_o, kv_norm_g, b_w_kv, b_w_q, b_w_o, ffn_w_up, ffn_conv_w, ffn_conv_b, ffn_w_down):
    batch, seq, d_model = x.shape
    depth = attn_norm_g.shape[0]
    n_a = a_w_qkv.shape[0]
    a_heads = d_model // (2 * HEAD_DIM)
    b_heads = d_model // HEAD_DIM
    qk_width = a_heads * 2 * HEAD_DIM
    scale = HEAD_DIM ** -0.5

    h = x.reshape(batch * seq, d_model)
    tables = _rotary_tables(positions)
    order = _head_dim_order()
    kp = vt = None
    for layer in range(depth):
        if layer < n_a:
            lambda_init = 0.8 - 0.6 * math.exp(-0.3 * layer)
            n_qk_heads = 2 * qk_width // HEAD_DIM
            col_order = (jnp.arange(n_qk_heads)[:, None] * HEAD_DIM + order[None, :]).reshape(-1)
            head_gain = jnp.concatenate([
                jnp.tile(a_q_norm_g[layer][order] * (scale * math.log2(math.e)), qk_width // HEAD_DIM),
                jnp.tile(a_k_norm_g[layer][order], qk_width // HEAD_DIM)]).reshape(1, -1)
            w_qkv = a_w_qkv[layer]
            qk, vt_a = _qkv_proj(h, attn_norm_g[layer], w_qkv[:, col_order].astype(BF16),
                                 w_qkv[:, 2 * qk_width:].T.astype(BF16), head_gain, tables,
                                 name=f"a{layer}_qkv")
            o = _diff_attention(qk, vt_a, a_lambda_q1[layer], a_lambda_k1[layer], a_lambda_q2[layer],
                                a_lambda_k2[layer], a_subln_g[layer], batch=batch, seq=seq,
                                heads=a_heads, lambda_init=lambda_init, name=f"a{layer}_attn")
            h = _proj_residual(o, a_w_o[layer].astype(BF16), h, bn=BN, name=f"a{layer}_out")
        else:
            if layer == n_a:
                b_width = b_heads * HEAD_DIM
                kp, vt = _kv_proj(h, kv_norm_g, b_w_kv[:, :b_width].astype(BF16),
                                  b_w_kv[:, b_width:].T.astype(BF16), name="b_kv")
            j = layer - n_a
            q = _norm_proj(h, attn_norm_g[layer], b_w_q[j].astype(BF16), out_scale=0.5 * scale,
                           name=f"b{j}_q")
            o = _stick_breaking(q, kp, vt, batch=batch, seq=seq, heads=b_heads, name=f"b{j}_attn")
            h = _proj_residual(o, b_w_o[j].astype(BF16), h, bn=BN, name=f"b{j}_out")
        act = _ffn_up(h, ffn_norm_g[layer], ffn_w_up[layer].astype(BF16), ffn_conv_w[layer],
                      ffn_conv_b[layer], seq=seq, name=f"l{layer}_ffn_up")
        h = _proj_residual(act, ffn_w_down[layer].astype(BF16), h, bn=BN_DOWN, name=f"l{layer}_ffn_down")
    return h.reshape(batch, seq, d_model)
```

```python
import functools
import math

import jax
import jax.numpy as jnp
from jax import lax
from jax.experimental import pallas as pl
from jax.experimental.pallas import tpu as pltpu

F32 = jnp.float32
BF16 = jnp.bfloat16

CHUNK = 64
HEAD_DIM = 128
ROT_DIM = HEAD_DIM // 4
ROT_PARTNER = HEAD_DIM // 2
ROPE_THETA = 500000.0
CONV_WIDTH = 3
EPS = 1e-6

LANES = 128
SUBLANES = 8
VMEM_LIMIT_BYTES = 56 * 1024 * 1024

BM = 1024
BN = 1024
BN_FFN = 512
BN_DOWN = 512
TQ = 256
TK = 256
DIFF_HEADS_PER_STEP = 8
STICK_HEADS_PER_STEP = 16
LOOKAHEAD = 3


def _params(*sem):
    return pltpu.CompilerParams(dimension_semantics=sem, vmem_limit_bytes=VMEM_LIMIT_BYTES)


def _rms_rows(x, gain):
    ms = jnp.mean(x * x, axis=-1, keepdims=True)
    return x * lax.rsqrt(ms + EPS) * gain


def _head_dim_order():
    half = ROT_DIM // 2
    return jnp.concatenate([jnp.arange(0, half), jnp.arange(ROT_DIM, ROT_PARTNER + half),
                            jnp.arange(half, ROT_DIM), jnp.arange(ROT_PARTNER + half, HEAD_DIM)])


def _rotary_table_kernel(pos_ref, inv_ref, c_ref, s_ref):
    ang = pos_ref[...] * inv_ref[...]
    cos = jnp.cos(ang)
    sin = jnp.sin(ang)
    lane = lax.broadcasted_iota(jnp.int32, ang.shape, 1)
    half = ROT_DIM // 2
    lo = lane < half
    hi = (lane >= ROT_PARTNER) & (lane < ROT_PARTNER + half)
    c_ref[...] = jnp.where(lo | hi, cos, 1.0)
    s_ref[...] = jnp.where(lo, -sin, jnp.where(hi, sin, 0.0))


def _rotary_tables(positions):
    tokens = positions.size
    bm = min(BM, tokens)
    pos = positions.reshape(tokens, 1).astype(F32)
    inv_freq = ROPE_THETA ** (-jnp.arange(0, ROT_DIM, 2, dtype=F32) / ROT_DIM)
    half = ROT_DIM // 2
    inv_lane = (jnp.zeros((1, HEAD_DIM), F32).at[0, :half].set(inv_freq)
                .at[0, ROT_PARTNER:ROT_PARTNER + half].set(inv_freq))
    spec = pl.BlockSpec((bm, HEAD_DIM), lambda i: (i, 0))
    out = jax.ShapeDtypeStruct((tokens, HEAD_DIM), F32)
    return pl.pallas_call(
        _rotary_table_kernel,
        grid=(tokens // bm,),
        in_specs=[pl.BlockSpec((bm, 1), lambda i: (i, 0)),
                  pl.BlockSpec((1, HEAD_DIM), lambda i: (0, 0))],
        out_specs=[spec, spec],
        out_shape=[out, out],
        compiler_params=_params("arbitrary"),
        name="rotary_tables",
    )(pos, inv_lane)


def _norm_proj_kernel(x_ref, g_ref, w_ref, o_ref, xn_ref, *, out_scale):
    @pl.when(pl.program_id(1) == 0)
    def _():
        xn_ref[...] = _rms_rows(x_ref[...], g_ref[...]).astype(BF16)

    acc = jnp.dot(xn_ref[...], w_ref[...], preferred_element_type=F32)
    if out_scale != 1.0:
        acc = acc * out_scale
    o_ref[...] = acc.astype(o_ref.dtype)


def _norm_proj(x, gain, w, *, out_scale=1.0, name):
    tokens, d = x.shape
    n = w.shape[1]
    bm, bn = min(BM, tokens), min(BN, n)
    assert tokens % bm == 0 and n % bn == 0, (tokens, bm, n, bn)
    return pl.pallas_call(
        functools.partial(_norm_proj_kernel, out_scale=out_scale),
        grid=(tokens // bm, n // bn),
        in_specs=[pl.BlockSpec((bm, d), lambda i, j: (i, 0)),
                  pl.BlockSpec((1, d), lambda i, j: (0, 0)),
                  pl.BlockSpec((d, bn), lambda i, j: (0, j))],
        out_specs=pl.BlockSpec((bm, bn), lambda i, j: (i, j)),
        out_shape=jax.ShapeDtypeStruct((tokens, n), BF16),
        scratch_shapes=[pltpu.VMEM((bm, d), BF16)],
        compiler_params=_params("arbitrary", "arbitrary"),
        name=name,
    )(x, gain.reshape(1, d), w)


def _qkv_kernel(x_ref, g_ref, w_ref, wvt_ref, hg_ref, c_ref, s_ref, qk_ref, vt_ref, xn_ref, acc_ref,
                *, n_qk_blocks):
    j = pl.program_id(1)

    @pl.when(j == 0)
    def _():
        xn_ref[...] = _rms_rows(x_ref[...], g_ref[...]).astype(BF16)

    @pl.when(j < n_qk_blocks)
    def _():
        acc_ref[...] = jnp.dot(xn_ref[...], w_ref[...], preferred_element_type=F32)

    @pl.when(j >= n_qk_blocks)
    def _():
        acc = lax.dot_general(wvt_ref[...], xn_ref[...], (((1,), (1,)), ((), ())),
                              preferred_element_type=F32)
        for t in range(vt_ref.shape[0]):
            vt_ref[t] = acc[:, t * TK:(t + 1) * TK].astype(vt_ref.dtype)

    @pl.when(j < n_qk_blocks)
    def _():
        c, sn = c_ref[...], s_ref[...]
        for h in range(acc_ref.shape[1] // HEAD_DIM):
            cols = slice(h * HEAD_DIM, (h + 1) * HEAD_DIM)
            y = _rms_rows(acc_ref[:, cols], hg_ref[:, cols])
            rot = y * c + pltpu.roll(y, ROT_PARTNER, 1) * sn
            qk_ref[:, cols] = rot.astype(qk_ref.dtype)


def _qkv_proj(x, gain, w_qk, w_vt, head_gain, tables, *, name):
    tokens, d = x.shape
    n_qk, n_v = w_qk.shape[1], w_vt.shape[0]
    bm, bn = min(BM, tokens), min(BN, n_v)
    assert tokens % bm == 0 and n_qk % bn == 0 and n_v % bn == 0 and bm % TK == 0, (tokens, bm, n_qk, n_v, bn)
    nqk = n_qk // bn
    tab = pl.BlockSpec((bm, HEAD_DIM), lambda i, j: (i, 0))
    qk_col = lambda i, j: (0, jnp.minimum(j, nqk - 1))
    return pl.pallas_call(
        functools.partial(_qkv_kernel, n_qk_blocks=nqk),
        grid=(tokens // bm, nqk + n_v // bn),
        in_specs=[pl.BlockSpec((bm, d), lambda i, j: (i, 0)),
                  pl.BlockSpec((1, d), lambda i, j: (0, 0)),
                  pl.BlockSpec((d, bn), qk_col),
                  pl.BlockSpec((bn, d), lambda i, j: (jnp.maximum(j - nqk, 0), 0)),
                  pl.BlockSpec((1, bn), qk_col),
                  tab, tab],
        out_specs=[pl.BlockSpec((bm, bn), lambda i, j: (i, jnp.minimum(j, nqk - 1))),
                   pl.BlockSpec((bm // TK, bn, TK), lambda i, j: (i, jnp.maximum(j - nqk, 0), 0))],
        out_shape=[jax.ShapeDtypeStruct((tokens, n_qk), BF16),
                   jax.ShapeDtypeStruct((tokens // TK, n_v, TK), BF16)],
        scratch_shapes=[pltpu.VMEM((bm, d), BF16),
                        pltpu.VMEM((bm, bn), F32)],
        compiler_params=_params("arbitrary", "arbitrary"),
        name=name,
    )(x, gain.reshape(1, d), w_qk, w_vt, head_gain, *tables)


def _proj_residual_kernel(x_ref, w_ref, r_ref, o_ref):
    o_ref[...] = r_ref[...] + jnp.dot(x_ref[...], w_ref[...], preferred_element_type=F32)


def _proj_residual(x, w, res, *, bn, name):
    tokens, k = x.shape
    n = w.shape[1]
    bm, bn = min(BM, tokens), min(bn, n)
    assert tokens % bm == 0 and n % bn == 0, (tokens, bm, n, bn)
    return pl.pallas_call(
        _proj_residual_kernel,
        grid=(tokens // bm, n // bn),
        in_specs=[pl.BlockSpec((bm, k), lambda i, j: (i, 0)),
                  pl.BlockSpec((k, bn), lambda i, j: (0, j)),
                  pl.BlockSpec((bm, bn), lambda i, j: (i, j))],
        out_specs=pl.BlockSpec((bm, bn), lambda i, j: (i, j)),
        out_shape=jax.ShapeDtypeStruct((tokens, n), F32),
        compiler_params=_params("arbitrary", "arbitrary"),
        name=name,
    )(x, w, res)


def _ffn_up_kernel(x_ref, g_ref, wu_ref, wg_ref, cw_ref, cb_ref, o_ref, xn_ref, halo_ref, gbuf_ref,
                   *, blocks_per_seq):
    i, j = pl.program_id(0), pl.program_id(1)

    @pl.when(j == 0)
    def _():
        xn_ref[...] = _rms_rows(x_ref[...], g_ref[...]).astype(BF16)

    xn = xn_ref[...]
    u = jnp.dot(xn, wu_ref[...], preferred_element_type=F32)
    g = jnp.dot(xn, wg_ref[...], preferred_element_type=F32)
    bm = g.shape[0]

    gbuf_ref[0:SUBLANES, :] = jnp.where(i % blocks_per_seq == 0, 0.0, halo_ref[j])
    gbuf_ref[SUBLANES:, :] = g
    halo_ref[j] = g[bm - SUBLANES:, :]
    g1 = gbuf_ref[SUBLANES - 1:SUBLANES - 1 + bm, :]
    g2 = gbuf_ref[SUBLANES - 2:SUBLANES - 2 + bm, :]
    hc = cw_ref[2:3, :] * g + cw_ref[1:2, :] * g1 + cw_ref[0:1, :] * g2 + cb_ref[...]
    act = (hc * u) * (1.0 + jnp.tanh(hc))
    o_ref[...] = act.astype(o_ref.dtype)


def _ffn_up(x, gain, w_up, conv_w, conv_b, *, seq, name):
    tokens, d = x.shape
    d_ff = w_up.shape[1] // 2
    bm = min(BM, seq)
    bn = min(BN_FFN, d_ff)
    assert tokens % bm == 0 and d_ff % bn == 0, (tokens, bm, d_ff, bn)
    nj = d_ff // bn
    return pl.pallas_call(
        functools.partial(_ffn_up_kernel, blocks_per_seq=seq // bm),
        grid=(tokens // bm, nj),
        in_specs=[pl.BlockSpec((bm, d), lambda i, j: (i, 0)),
                  pl.BlockSpec((1, d), lambda i, j: (0, 0)),
                  pl.BlockSpec((d, bn), lambda i, j: (0, j)),
                  pl.BlockSpec((d, bn), lambda i, j: (0, j + nj)),
                  pl.BlockSpec((CONV_WIDTH, bn), lambda i, j: (0, j)),
                  pl.BlockSpec((1, bn), lambda i, j: (0, j))],
        out_specs=pl.BlockSpec((bm, bn), lambda i, j: (i, j)),
        out_shape=jax.ShapeDtypeStruct((tokens, d_ff), BF16),
        scratch_shapes=[pltpu.VMEM((bm, d), BF16),
                        pltpu.VMEM((nj, SUBLANES, bn), F32),
                        pltpu.VMEM((bm + SUBLANES, bn), F32)],
        compiler_params=_params("arbitrary", "arbitrary"),
        name=name,
    )(x, gain.reshape(1, d), w_up, w_up, 0.5 * conv_w, 0.5 * conv_b.reshape(1, d_ff))


def _all_sublanes(x, op):
    for d in (1, 2, 4):
        x = op(x, pltpu.roll(x, d, 0))
    return x


def _diff_attn_kernel(lq1_ref, lk1_ref, lq2_ref, lk2_ref, sg_ref, q_ref, k_ref, vt_ref, o_ref,
                      acc_ref, qt_ref, sc_ref, pc_ref, ac_ref, *, lambda_init, heads_per_step, lookahead):
    i = pl.program_id(2)
    tq = q_ref.shape[0]
    nv = TK // SUBLANES
    width = 2 * HEAD_DIM
    units = 2 * heads_per_step

    row = lax.broadcasted_iota(jnp.int32, (TK, tq), 0)
    col = lax.broadcasted_iota(jnp.int32, (TK, tq), 1)
    chunk_mask = (row // CHUNK) <= (col // CHUNK)

    acc_ref[...] = jnp.zeros_like(acc_ref)
    for u in range(units):
        qt_ref[u] = q_ref[:, u * HEAD_DIM:(u + 1) * HEAD_DIM].T

    def scores(kt, u):
        start = pl.multiple_of(kt * TK, TK)
        k = k_ref[pl.ds(start, TK), u * HEAD_DIM:(u + 1) * HEAD_DIM]
        return jnp.dot(k, qt_ref[u], preferred_element_type=F32)

    def softmax_step(s, m, lpart, masked):
        if masked:
            s = jnp.where(chunk_mask, s, -jnp.inf)
        p_cols, m_cols, l_cols, a_cols = [], [], [], []
        for ch in range(tq // LANES):
            lanes = slice(ch * LANES, (ch + 1) * LANES)
            s3 = s[:, lanes].reshape(nv, SUBLANES, LANES)
            tmax = s3[0]
            for v in range(1, nv):
                tmax = jnp.maximum(tmax, s3[v])
            m_new = jnp.maximum(m[:, lanes], _all_sublanes(tmax, jnp.maximum))
            alpha = jnp.exp2(m[:, lanes] - m_new)
            p3 = [jnp.exp2(s3[v] - m_new) for v in range(nv)]
            psum = p3[0]
            for v in range(1, nv):
                psum = psum + p3[v]
            p_cols.append(jnp.concatenate(p3, axis=0))
            m_cols.append(m_new)
            l_cols.append(alpha * lpart[:, lanes] + psum)
            a_cols.append(alpha)
        cat = lambda xs: jnp.concatenate(xs, axis=1)
        return cat(p_cols).astype(BF16), cat(m_cols), cat(l_cols), cat(a_cols)

    def rows(x):
        return jnp.concatenate([x] * (width // SUBLANES), axis=0)

    def accumulate(u, kt, alpha, p):
        hd = u // 2
        acc_ref[u] = rows(alpha) * acc_ref[u] + jnp.dot(vt_ref[kt, hd * width:(hd + 1) * width, :], p,
                                                        preferred_element_type=F32)

    def tile(kt, kt_next, carry, masked, first):
        if not first:
            accumulate(units - 1, kt + 1, ac_ref[...], pc_ref[...])
        new = []
        ahead = [sc_ref[d] for d in range(lookahead)]
        for u in range(units):
            s = ahead.pop(0)
            ahead.append(scores(kt, u + lookahead) if u + lookahead < units
                         else scores(kt_next, u + lookahead - units))
            p, m, l, a = softmax_step(s, carry[2 * u], carry[2 * u + 1], masked)
            new += [m, l]
            if u + 1 < units:
                accumulate(u, kt, a, p)
            else:
                pc_ref[...] = p
                ac_ref[...] = a
        for d in range(lookahead):
            sc_ref[d] = ahead[d]
        return tuple(new)

    for d in range(lookahead):
        sc_ref[d] = scores(i, d)
    neg = jnp.full((SUBLANES, tq), -jnp.inf, F32)
    zero = jnp.zeros((SUBLANES, tq), F32)
    carry = tile(i, jnp.maximum(i - 1, 0), (neg, zero) * units, True, True)
    carry = lax.fori_loop(
        0, i, lambda s, cr: tile(i - 1 - s, jnp.maximum(i - 2 - s, 0), cr, False, False), carry)
    accumulate(units - 1, 0, ac_ref[...], pc_ref[...])

    lam = (jnp.exp(jnp.sum(lq1_ref[...] * lk1_ref[...], axis=1, keepdims=True))
           - jnp.exp(jnp.sum(lq2_ref[...] * lk2_ref[...], axis=1, keepdims=True)) + lambda_init)
    for hd in range(heads_per_step):
        inv0 = 1.0 / _all_sublanes(carry[4 * hd + 1], jnp.add)
        inv1 = 1.0 / _all_sublanes(carry[4 * hd + 3], jnp.add)
        o_t = acc_ref[2 * hd] * rows(inv0) - lam * (acc_ref[2 * hd + 1] * rows(inv1))
        o = _rms_rows(o_t.T, sg_ref[...]) * (1.0 - lambda_init)
        o_ref[:, hd * width:(hd + 1) * width] = o.astype(o_ref.dtype)


def _diff_attention(qk, vt, lq1, lk1, lq2, lk2, subln_g, *, batch, seq, heads, lambda_init, name):
    width = 2 * HEAD_DIM
    hp = min(DIFF_HEADS_PER_STEP, heads)
    groups = heads // hp
    qk3 = qk.reshape(batch, seq, 2 * heads * width)
    vt = vt.reshape(batch, seq // TK, heads * width, TK)
    vec = pl.BlockSpec((1, HEAD_DIM), lambda b, h, i: (0, 0))
    out = pl.pallas_call(
        functools.partial(_diff_attn_kernel, lambda_init=lambda_init, heads_per_step=hp,
                          lookahead=min(LOOKAHEAD, 2 * hp)),
        grid=(batch, groups, seq // TQ),
        in_specs=[vec, vec, vec, vec,
                  pl.BlockSpec((1, width), lambda b, h, i: (0, 0)),
                  pl.BlockSpec((None, TQ, hp * width), lambda b, h, i: (b, i, h)),
                  pl.BlockSpec((None, seq, hp * width), lambda b, h, i: (b, 0, groups + h)),
                  pl.BlockSpec((None, seq // TK, hp * width, TK), lambda b, h, i: (b, 0, h, 0))],
        out_specs=pl.BlockSpec((None, TQ, hp * width), lambda b, h, i: (b, i, h)),
        out_shape=jax.ShapeDtypeStruct((batch, seq, heads * width), BF16),
        scratch_shapes=[pltpu.VMEM((2 * hp, width, TQ), F32),
                        pltpu.VMEM((2 * hp, HEAD_DIM, TQ), BF16),
                        pltpu.VMEM((LOOKAHEAD, TK, TQ), F32),
                        pltpu.VMEM((TK, TQ), BF16),
                        pltpu.VMEM((SUBLANES, TQ), F32)],
        compiler_params=_params("arbitrary", "arbitrary", "arbitrary"),
        name=name,
    )(lq1.reshape(1, -1), lk1.reshape(1, -1), lq2.reshape(1, -1), lk2.reshape(1, -1),
      subln_g.reshape(1, -1), qk3, qk3, vt)
    return out.reshape(batch * seq, heads * width)


def _stick_kernel(q_ref, k_ref, vt_ref, o_ref, acc_ref, qt_ref, zc_ref, ac_ref, *, heads_per_step, lookahead):
    i = pl.program_id(2)
    tq = q_ref.shape[0]
    nv = TK // SUBLANES
    hp = heads_per_step
    last = slice((hp - 1) * HEAD_DIM, hp * HEAD_DIM)

    row = lax.broadcasted_iota(jnp.int32, (TK, tq), 0)
    col = lax.broadcasted_iota(jnp.int32, (TK, tq), 1)
    causal = ((row % SUBLANES) * nv + row // SUBLANES) < col
    sub = lax.broadcasted_iota(jnp.int32, (SUBLANES, LANES), 0)

    acc_ref[...] = jnp.zeros_like(acc_ref)
    for hh in range(hp):
        qt_ref[hh] = q_ref[:, hh * HEAD_DIM:(hh + 1) * HEAD_DIM].T

    def scores(kt, hh):
        start = pl.multiple_of(kt * TK, TK)
        k = k_ref[pl.ds(start, TK), hh * HEAD_DIM:(hh + 1) * HEAD_DIM]
        return jnp.dot(k, qt_ref[hh], preferred_element_type=F32)

    def weights(zh, right, masked):
        beta = 0.5 * jnp.tanh(zh) + 0.5
        if masked:
            beta = jnp.where(causal, beta, 0.0)
        a_cols, totals = [], []
        for ch in range(tq // LANES):
            lanes = slice(ch * LANES, (ch + 1) * LANES)
            beta3 = beta[:, lanes].reshape(nv, SUBLANES, LANES)
            run = jnp.ones((SUBLANES, LANES), F32)
            part = [None] * nv
            for v in range(nv - 1, -1, -1):
                part[v] = beta3[v] * run
                run = run - part[v]
            t = run
            for d in (1, 2, 4):
                shifted = pltpu.roll(t, SUBLANES - d, 0)
                t = t * jnp.where(sub + d < SUBLANES, shifted, 1.0)
            later = jnp.where(sub + 1 < SUBLANES, pltpu.roll(t, SUBLANES - 1, 0), 1.0)
            off = right[:, lanes] * later
            a_cols.append(jnp.concatenate([part[v] * off for v in range(nv)], axis=0))
            totals.append(jnp.broadcast_to(t[0:1, :], (SUBLANES, LANES)))
        a = jnp.concatenate(a_cols, axis=1).astype(BF16)
        return a, right * jnp.concatenate(totals, axis=1)

    def tile(kt, kt_next, right, masked, first):
        if not first:
            acc_ref[hp - 1] += jnp.dot(vt_ref[kt + 1, last, :], ac_ref[...], preferred_element_type=F32)
        new_right = []
        ahead = [zc_ref[d] for d in range(lookahead)]
        for hh in range(hp):
            z = ahead.pop(0)
            ahead.append(scores(kt, hh + lookahead) if hh + lookahead < hp
                         else scores(kt_next, hh + lookahead - hp))
            a, r = weights(z, right[hh], masked)
            new_right.append(r)
            if hh + 1 < hp:
                acc_ref[hh] += jnp.dot(vt_ref[kt, hh * HEAD_DIM:(hh + 1) * HEAD_DIM, :], a,
                                       preferred_element_type=F32)
            else:
                ac_ref[...] = a
        for d in range(lookahead):
            zc_ref[d] = ahead[d]
        return tuple(new_right)

    for d in range(lookahead):
        zc_ref[d] = scores(i, d)
    ones = tuple(jnp.ones((SUBLANES, tq), F32) for _ in range(hp))
    right = tile(i, jnp.maximum(i - 1, 0), ones, True, True)
    lax.fori_loop(0, i, lambda s, rt: tile(i - 1 - s, jnp.maximum(i - 2 - s, 0), rt, False, False), right)
    acc_ref[hp - 1] += jnp.dot(vt_ref[0, last, :], ac_ref[...], preferred_element_type=F32)
    for hh in range(hp):
        o_ref[:, hh * HEAD_DIM:(hh + 1) * HEAD_DIM] = acc_ref[hh].T.astype(o_ref.dtype)


def _stick_breaking(q, kp, vt, *, batch, seq, heads, name):
    width = heads * HEAD_DIM
    hp = min(STICK_HEADS_PER_STEP, heads)
    q3 = q.reshape(batch, seq, width)
    kp = kp.reshape(batch, seq, width)
    vt = vt.reshape(batch, seq // TK, width, TK)
    out = pl.pallas_call(
        functools.partial(_stick_kernel, heads_per_step=hp, lookahead=min(LOOKAHEAD, hp)),
        grid=(batch, heads // hp, seq // TQ),
        in_specs=[pl.BlockSpec((None, TQ, hp * HEAD_DIM), lambda b, h, i: (b, i, h)),
                  pl.BlockSpec((None, seq, hp * HEAD_DIM), lambda b, h, i: (b, 0, h)),
                  pl.BlockSpec((None, seq // TK, hp * HEAD_DIM, TK), lambda b, h, i: (b, 0, h, 0))],
        out_specs=pl.BlockSpec((None, TQ, hp * HEAD_DIM), lambda b, h, i: (b, i, h)),
        out_shape=jax.ShapeDtypeStruct((batch, seq, width), BF16),
        scratch_shapes=[pltpu.VMEM((hp, HEAD_DIM, TQ), F32),
                        pltpu.VMEM((hp, HEAD_DIM, TQ), BF16),
                        pltpu.VMEM((LOOKAHEAD, TK, TQ), F32),
                        pltpu.VMEM((TK, TQ), BF16)],
        compiler_params=_params("arbitrary", "arbitrary", "arbitrary"),
        name=name,
    )(q3, kp, vt)
    return out.reshape(batch * seq, width)


def _kv_kernel(x_ref, g_ref, wk_ref, wvt_ref, k_ref, vt_ref, xn_ref, *, n_k_blocks):
    j = pl.program_id(1)
    bm = x_ref.shape[0]
    nv = TK // SUBLANES

    @pl.when(j == 0)
    def _():
        xn = _rms_rows(x_ref[...], g_ref[...]).astype(BF16)
        r = lax.broadcasted_iota(jnp.int32, (TK, TK), 0)
        t = lax.broadcasted_iota(jnp.int32, (TK, TK), 1)
        pick = jnp.where(t == (r % SUBLANES) * nv + r // SUBLANES, 1.0, 0.0).astype(BF16)
        for grp in range(bm // TK):
            rows = slice(grp * TK, (grp + 1) * TK)
            xn_ref[rows, :] = jnp.dot(pick, xn[rows, :], preferred_element_type=F32).astype(BF16)

    @pl.when(j < n_k_blocks)
    def _():
        k_ref[...] = jnp.dot(xn_ref[...], wk_ref[...], preferred_element_type=F32).astype(k_ref.dtype)

    @pl.when(j >= n_k_blocks)
    def _():
        acc = lax.dot_general(wvt_ref[...], xn_ref[...], (((1,), (1,)), ((), ())),
                              preferred_element_type=F32)
        for t in range(vt_ref.shape[0]):
            vt_ref[t] = acc[:, t * TK:(t + 1) * TK].astype(vt_ref.dtype)


def _kv_proj(x, gain, wk, wvt, *, name):
    tokens, d = x.shape
    n = wk.shape[1]
    bm, bn = min(BM, tokens), min(BN, n)
    assert tokens % bm == 0 and n % bn == 0 and bm % TK == 0, (tokens, bm, n, bn)
    nk = n // bn
    return pl.pallas_call(
        functools.partial(_kv_kernel, n_k_blocks=nk),
        grid=(tokens // bm, 2 * nk),
        in_specs=[pl.BlockSpec((bm, d), lambda i, j: (i, 0)),
                  pl.BlockSpec((1, d), lambda i, j: (0, 0)),
                  pl.BlockSpec((d, bn), lambda i, j: (0, jnp.minimum(j, nk - 1))),
                  pl.BlockSpec((bn, d), lambda i, j: (jnp.maximum(j - nk, 0), 0))],
        out_specs=[pl.BlockSpec((bm, bn), lambda i, j: (i, jnp.minimum(j, nk - 1))),
                   pl.BlockSpec((bm // TK, bn, TK), lambda i, j: (i, jnp.maximum(j - nk, 0), 0))],
        out_shape=[jax.ShapeDtypeStruct((tokens, n), BF16),
                   jax.ShapeDtypeStruct((tokens // TK, n, TK), BF16)],
        scratch_shapes=[pltpu.VMEM((bm, d), BF16)],
        compiler_params=_params("arbitrary", "arbitrary"),
        name=name,
    )(x, gain.reshape(1, d), wk, wvt)


def kernel(x, positions, attn_norm_g, ffn_norm_g, a_w_qkv, a_q_norm_g, a_k_norm_g, a_lambda_q1, a_lambda_k1, a_lambda_q2, a_lambda_k2, a_subln_g, a_w_o, kv_norm_g, b_w_kv, b_w_q, b_w_o, ffn_w_up, ffn_conv_w, ffn_conv_b, ffn_w_down):
    batch, seq, d_model = x.shape
    depth = attn_norm_g.shape[0]
    n_a = a_w_qkv.shape[0]
    a_heads = d_model // (2 * HEAD_DIM)
    b_heads = d_model // HEAD_DIM
    qk_width = a_heads * 2 * HEAD_DIM
    scale = HEAD_DIM ** -0.5

    h = x.reshape(batch * seq, d_model)
    tables = _rotary_tables(positions)
    order = _head_dim_order()
    kp = vt = None
    for layer in range(depth):
        if layer < n_a:
            lambda_init = 0.8 - 0.6 * math.exp(-0.3 * layer)
            n_qk_heads = 2 * qk_width // HEAD_DIM
            col_order = (jnp.arange(n_qk_heads)[:, None] * HEAD_DIM + order[None, :]).reshape(-1)
            head_gain = jnp.concatenate([
                jnp.tile(a_q_norm_g[layer][order] * (scale * math.log2(math.e)), qk_width // HEAD_DIM),
                jnp.tile(a_k_norm_g[layer][order], qk_width // HEAD_DIM)]).reshape(1, -1)
            w_qkv = a_w_qkv[layer]
            qk, vt_a = _qkv_proj(h, attn_norm_g[layer], w_qkv[:, col_order].astype(BF16),
                                 w_qkv[:, 2 * qk_width:].T.astype(BF16), head_gain, tables,
                                 name=f"a{layer}_qkv")
            o = _diff_attention(qk, vt_a, a_lambda_q1[layer], a_lambda_k1[layer], a_lambda_q2[layer],
                                a_lambda_k2[layer], a_subln_g[layer], batch=batch, seq=seq,
                                heads=a_heads, lambda_init=lambda_init, name=f"a{layer}_attn")
            h = _proj_residual(o, a_w_o[layer].astype(BF16), h, bn=BN, name=f"a{layer}_out")
        else:
            if layer == n_a:
                b_width = b_heads * HEAD_DIM
                kp, vt = _kv_proj(h, kv_norm_g, b_w_kv[:, :b_width].astype(BF16),
                                  b_w_kv[:, b_width:].T.astype(BF16), name="b_kv")
            j = layer - n_a
            q = _norm_proj(h, attn_norm_g[layer], b_w_q[j].astype(BF16), out_scale=0.5 * scale,
                           name=f"b{j}_q")
            o = _stick_breaking(q, kp, vt, batch=batch, seq=seq, heads=b_heads, name=f"b{j}_attn")
            h = _proj_residual(o, b_w_o[j].astype(BF16), h, bn=BN, name=f"b{j}_out")
        act = _ffn_up(h, ffn_norm_g[layer], ffn_w_up[layer].astype(BF16), ffn_conv_w[layer],
                      ffn_conv_b[layer], seq=seq, name=f"l{layer}_ffn_up")
        h = _proj_residual(act, ffn_w_down[layer].astype(BF16), h, bn=BN_DOWN, name=f"l{layer}_ffn_down")
    return h.reshape(batch, seq, d_model)
```

```python
import functools
import math

import jax
import jax.numpy as jnp
from jax import lax
from jax.experimental import pallas as pl
from jax.experimental.pallas import tpu as pltpu

F32 = jnp.float32
BF16 = jnp.bfloat16

CHUNK = 64
HEAD_DIM = 128
ROT_DIM = HEAD_DIM // 4
ROT_PARTNER = HEAD_DIM // 2
ROPE_THETA = 500000.0
CONV_WIDTH = 3
EPS = 1e-6

LANES = 128
SUBLANES = 8
VMEM_LIMIT_BYTES = 56 * 1024 * 1024

BM = 1024
BN = 1024
BN_FFN = 512
BN_DOWN = 512
TQ = 256
TK = 256
DIFF_HEADS_PER_STEP = 8
STICK_HEADS_PER_STEP = 16
LOOKAHEAD = 3
BM_SQUARE = 512
BN_SQUARE = 2048


def _params(*sem):
    return pltpu.CompilerParams(dimension_semantics=sem, vmem_limit_bytes=VMEM_LIMIT_BYTES)


def _rms_rows(x, gain):
    ms = jnp.mean(x * x, axis=-1, keepdims=True)
    return x * lax.rsqrt(ms + EPS) * gain


def _head_dim_order():
    half = ROT_DIM // 2
    return jnp.concatenate([jnp.arange(0, half), jnp.arange(ROT_DIM, ROT_PARTNER + half),
                            jnp.arange(half, ROT_DIM), jnp.arange(ROT_PARTNER + half, HEAD_DIM)])


def _rotary_table_kernel(pos_ref, inv_ref, c_ref, s_ref):
    ang = pos_ref[...] * inv_ref[...]
    cos = jnp.cos(ang)
    sin = jnp.sin(ang)
    lane = lax.broadcasted_iota(jnp.int32, ang.shape, 1)
    half = ROT_DIM // 2
    lo = lane < half
    hi = (lane >= ROT_PARTNER) & (lane < ROT_PARTNER + half)
    c_ref[...] = jnp.where(lo | hi, cos, 1.0)
    s_ref[...] = jnp.where(lo, -sin, jnp.where(hi, sin, 0.0))


def _rotary_tables(positions):
    tokens = positions.size
    bm = min(BM, tokens)
    pos = positions.reshape(tokens, 1).astype(F32)
    inv_freq = ROPE_THETA ** (-jnp.arange(0, ROT_DIM, 2, dtype=F32) / ROT_DIM)
    half = ROT_DIM // 2
    inv_lane = (jnp.zeros((1, HEAD_DIM), F32).at[0, :half].set(inv_freq)
                .at[0, ROT_PARTNER:ROT_PARTNER + half].set(inv_freq))
    spec = pl.BlockSpec((bm, HEAD_DIM), lambda i: (i, 0))
    out = jax.ShapeDtypeStruct((tokens, HEAD_DIM), F32)
    return pl.pallas_call(
        _rotary_table_kernel,
        grid=(tokens // bm,),
        in_specs=[pl.BlockSpec((bm, 1), lambda i: (i, 0)),
                  pl.BlockSpec((1, HEAD_DIM), lambda i: (0, 0))],
        out_specs=[spec, spec],
        out_shape=[out, out],
        compiler_params=_params("arbitrary"),
        name="rotary_tables",
    )(pos, inv_lane)


def _norm_proj_kernel(x_ref, g_ref, w_ref, o_ref, xn_ref, *, out_scale):
    @pl.when(pl.program_id(1) == 0)
    def _():
        xn_ref[...] = _rms_rows(x_ref[...], g_ref[...]).astype(BF16)

    acc = jnp.dot(xn_ref[...], w_ref[...], preferred_element_type=F32)
    if out_scale != 1.0:
        acc = acc * out_scale
    o_ref[...] = acc.astype(o_ref.dtype)


def _norm_proj(x, gain, w, *, out_scale=1.0, bm=BM, bn=BN, name):
    tokens, d = x.shape
    n = w.shape[1]
    bm, bn = min(bm, tokens), min(bn, n)
    assert tokens % bm == 0 and n % bn == 0, (tokens, bm, n, bn)
    return pl.pallas_call(
        functools.partial(_norm_proj_kernel, out_scale=out_scale),
        grid=(tokens // bm, n // bn),
        in_specs=[pl.BlockSpec((bm, d), lambda i, j: (i, 0)),
                  pl.BlockSpec((1, d), lambda i, j: (0, 0)),
                  pl.BlockSpec((d, bn), lambda i, j: (0, j))],
        out_specs=pl.BlockSpec((bm, bn), lambda i, j: (i, j)),
        out_shape=jax.ShapeDtypeStruct((tokens, n), BF16),
        scratch_shapes=[pltpu.VMEM((bm, d), BF16)],
        compiler_params=_params("arbitrary", "arbitrary"),
        name=name,
    )(x, gain.reshape(1, d), w)


def _qkv_kernel(x_ref, g_ref, w_ref, wvt_ref, hg_ref, c_ref, s_ref, qk_ref, vt_ref, xn_ref, acc_ref,
                *, n_qk_blocks):
    j = pl.program_id(1)

    @pl.when(j == 0)
    def _():
        xn_ref[...] = _rms_rows(x_ref[...], g_ref[...]).astype(BF16)

    @pl.when(j < n_qk_blocks)
    def _():
        acc_ref[...] = jnp.dot(xn_ref[...], w_ref[...], preferred_element_type=F32)

    @pl.when(j >= n_qk_blocks)
    def _():
        acc = lax.dot_general(wvt_ref[...], xn_ref[...], (((1,), (1,)), ((), ())),
                              preferred_element_type=F32)
        for t in range(vt_ref.shape[0]):
            vt_ref[t] = acc[:, t * TK:(t + 1) * TK].astype(vt_ref.dtype)

    @pl.when(j < n_qk_blocks)
    def _():
        c, sn = c_ref[...], s_ref[...]
        for h in range(acc_ref.shape[1] // HEAD_DIM):
            cols = slice(h * HEAD_DIM, (h + 1) * HEAD_DIM)
            y = _rms_rows(acc_ref[:, cols], hg_ref[:, cols])
            rot = y * c + pltpu.roll(y, ROT_PARTNER, 1) * sn
            qk_ref[:, cols] = rot.astype(qk_ref.dtype)


def _qkv_proj(x, gain, w_qk, w_vt, head_gain, tables, *, name):
    tokens, d = x.shape
    n_qk, n_v = w_qk.shape[1], w_vt.shape[0]
    bm, bn = min(BM, tokens), min(BN, n_v)
    assert tokens % bm == 0 and n_qk % bn == 0 and n_v % bn == 0 and bm % TK == 0, (tokens, bm, n_qk, n_v, bn)
    nqk = n_qk // bn
    tab = pl.BlockSpec((bm, HEAD_DIM), lambda i, j: (i, 0))
    qk_col = lambda i, j: (0, jnp.minimum(j, nqk - 1))
    return pl.pallas_call(
        functools.partial(_qkv_kernel, n_qk_blocks=nqk),
        grid=(tokens // bm, nqk + n_v // bn),
        in_specs=[pl.BlockSpec((bm, d), lambda i, j: (i, 0)),
                  pl.BlockSpec((1, d), lambda i, j: (0, 0)),
                  pl.BlockSpec((d, bn), qk_col),
                  pl.BlockSpec((bn, d), lambda i, j: (jnp.maximum(j - nqk, 0), 0)),
                  pl.BlockSpec((1, bn), qk_col),
                  tab, tab],
        out_specs=[pl.BlockSpec((bm, bn), lambda i, j: (i, jnp.minimum(j, nqk - 1))),
                   pl.BlockSpec((bm // TK, bn, TK), lambda i, j: (i, jnp.maximum(j - nqk, 0), 0))],
        out_shape=[jax.ShapeDtypeStruct((tokens, n_qk), BF16),
                   jax.ShapeDtypeStruct((tokens // TK, n_v, TK), BF16)],
        scratch_shapes=[pltpu.VMEM((bm, d), BF16),
                        pltpu.VMEM((bm, bn), F32)],
        compiler_params=_params("arbitrary", "arbitrary"),
        name=name,
    )(x, gain.reshape(1, d), w_qk, w_vt, head_gain, *tables)


def _proj_residual_kernel(x_ref, w_ref, r_ref, o_ref):
    o_ref[...] = r_ref[...] + jnp.dot(x_ref[...], w_ref[...], preferred_element_type=F32)


def _proj_residual(x, w, res, *, bm=BM, bn, name):
    tokens, k = x.shape
    n = w.shape[1]
    bm, bn = min(bm, tokens), min(bn, n)
    assert tokens % bm == 0 and n % bn == 0, (tokens, bm, n, bn)
    return pl.pallas_call(
        _proj_residual_kernel,
        grid=(tokens // bm, n // bn),
        in_specs=[pl.BlockSpec((bm, k), lambda i, j: (i, 0)),
                  pl.BlockSpec((k, bn), lambda i, j: (0, j)),
                  pl.BlockSpec((bm, bn), lambda i, j: (i, j))],
        out_specs=pl.BlockSpec((bm, bn), lambda i, j: (i, j)),
        out_shape=jax.ShapeDtypeStruct((tokens, n), F32),
        compiler_params=_params("arbitrary", "arbitrary"),
        name=name,
    )(x, w, res)


def _ffn_up_kernel(x_ref, g_ref, wu_ref, wg_ref, cw_ref, cb_ref, o_ref, xn_ref, halo_ref, gbuf_ref,
                   *, blocks_per_seq):
    i, j = pl.program_id(0), pl.program_id(1)

    @pl.when(j == 0)
    def _():
        xn_ref[...] = _rms_rows(x_ref[...], g_ref[...]).astype(BF16)

    xn = xn_ref[...]
    u = jnp.dot(xn, wu_ref[...], preferred_element_type=F32)
    g = jnp.dot(xn, wg_ref[...], preferred_element_type=F32)
    bm = g.shape[0]

    gbuf_ref[0:SUBLANES, :] = jnp.where(i % blocks_per_seq == 0, 0.0, halo_ref[j])
    gbuf_ref[SUBLANES:, :] = g
    halo_ref[j] = g[bm - SUBLANES:, :]
    g1 = gbuf_ref[SUBLANES - 1:SUBLANES - 1 + bm, :]
    g2 = gbuf_ref[SUBLANES - 2:SUBLANES - 2 + bm, :]
    hc = cw_ref[2:3, :] * g + cw_ref[1:2, :] * g1 + cw_ref[0:1, :] * g2 + cb_ref[...]
    act = (hc * u) * (1.0 + jnp.tanh(hc))
    o_ref[...] = act.astype(o_ref.dtype)


def _ffn_up(x, gain, w_up, conv_w, conv_b, *, seq, name):
    tokens, d = x.shape
    d_ff = w_up.shape[1] // 2
    bm = min(BM, seq)
    bn = min(BN_FFN, d_ff)
    assert tokens % bm == 0 and d_ff % bn == 0, (tokens, bm, d_ff, bn)
    nj = d_ff // bn
    return pl.pallas_call(
        functools.partial(_ffn_up_kernel, blocks_per_seq=seq // bm),
        grid=(tokens // bm, nj),
        in_specs=[pl.BlockSpec((bm, d), lambda i, j: (i, 0)),
                  pl.BlockSpec((1, d), lambda i, j: (0, 0)),
                  pl.BlockSpec((d, bn), lambda i, j: (0, j)),
                  pl.BlockSpec((d, bn), lambda i, j: (0, j + nj)),
                  pl.BlockSpec((CONV_WIDTH, bn), lambda i, j: (0, j)),
                  pl.BlockSpec((1, bn), lambda i, j: (0, j))],
        out_specs=pl.BlockSpec((bm, bn), lambda i, j: (i, j)),
        out_shape=jax.ShapeDtypeStruct((tokens, d_ff), BF16),
        scratch_shapes=[pltpu.VMEM((bm, d), BF16),
                        pltpu.VMEM((nj, SUBLANES, bn), F32),
                        pltpu.VMEM((bm + SUBLANES, bn), F32)],
        compiler_params=_params("arbitrary", "arbitrary"),
        name=name,
    )(x, gain.reshape(1, d), w_up, w_up, 0.5 * conv_w, 0.5 * conv_b.reshape(1, d_ff))


def _all_sublanes(x, op):
    for d in (1, 2, 4):
        x = op(x, pltpu.roll(x, d, 0))
    return x


def _diff_attn_kernel(lq1_ref, lk1_ref, lq2_ref, lk2_ref, sg_ref, q_ref, k_ref, vt_ref, o_ref,
                      acc_ref, qt_ref, sc_ref, pc_ref, ac_ref, *, lambda_init, heads_per_step, lookahead):
    i = pl.program_id(2)
    tq = q_ref.shape[0]
    nv = TK // SUBLANES
    width = 2 * HEAD_DIM
    units = 2 * heads_per_step

    row = lax.broadcasted_iota(jnp.int32, (TK, tq), 0)
    col = lax.broadcasted_iota(jnp.int32, (TK, tq), 1)
    chunk_mask = (row // CHUNK) <= (col // CHUNK)

    acc_ref[...] = jnp.zeros_like(acc_ref)
    for u in range(units):
        qt_ref[u] = q_ref[:, u * HEAD_DIM:(u + 1) * HEAD_DIM].T

    def scores(kt, u):
        start = pl.multiple_of(kt * TK, TK)
        k = k_ref[pl.ds(start, TK), u * HEAD_DIM:(u + 1) * HEAD_DIM]
        return jnp.dot(k, qt_ref[u], preferred_element_type=F32)

    def softmax_step(s, m, lpart, masked):
        if masked:
            s = jnp.where(chunk_mask, s, -jnp.inf)
        p_cols, m_cols, l_cols, a_cols = [], [], [], []
        for ch in range(tq // LANES):
            lanes = slice(ch * LANES, (ch + 1) * LANES)
            s3 = s[:, lanes].reshape(nv, SUBLANES, LANES)
            tmax = s3[0]
            for v in range(1, nv):
                tmax = jnp.maximum(tmax, s3[v])
            m_new = jnp.maximum(m[:, lanes], _all_sublanes(tmax, jnp.maximum))
            alpha = jnp.exp2(m[:, lanes] - m_new)
            p3 = [jnp.exp2(s3[v] - m_new) for v in range(nv)]
            psum = p3[0]
            for v in range(1, nv):
                psum = psum + p3[v]
            p_cols.append(jnp.concatenate(p3, axis=0))
            m_cols.append(m_new)
            l_cols.append(alpha * lpart[:, lanes] + psum)
            a_cols.append(alpha)
        cat = lambda xs: jnp.concatenate(xs, axis=1)
        return cat(p_cols).astype(BF16), cat(m_cols), cat(l_cols), cat(a_cols)

    def rows(x):
        return jnp.concatenate([x] * (width // SUBLANES), axis=0)

    def accumulate(u, kt, alpha, p):
        hd = u // 2
        acc_ref[u] = rows(alpha) * acc_ref[u] + jnp.dot(vt_ref[kt, hd * width:(hd + 1) * width, :], p,
                                                        preferred_element_type=F32)

    def tile(kt, kt_next, carry, masked, first):
        if not first:
            accumulate(units - 1, kt + 1, ac_ref[...], pc_ref[...])
        new = []
        ahead = [sc_ref[d] for d in range(lookahead)]
        for u in range(units):
            s = ahead.pop(0)
            ahead.append(scores(kt, u + lookahead) if u + lookahead < units
                         else scores(kt_next, u + lookahead - units))
            p, m, l, a = softmax_step(s, carry[2 * u], carry[2 * u + 1], masked)
            new += [m, l]
            if u + 1 < units:
                accumulate(u, kt, a, p)
            else:
                pc_ref[...] = p
                ac_ref[...] = a
        for d in range(lookahead):
            sc_ref[d] = ahead[d]
        return tuple(new)

    for d in range(lookahead):
        sc_ref[d] = scores(i, d)
    neg = jnp.full((SUBLANES, tq), -jnp.inf, F32)
    zero = jnp.zeros((SUBLANES, tq), F32)
    carry = tile(i, jnp.maximum(i - 1, 0), (neg, zero) * units, True, True)
    carry = lax.fori_loop(
        0, i, lambda s, cr: tile(i - 1 - s, jnp.maximum(i - 2 - s, 0), cr, False, False), carry)
    accumulate(units - 1, 0, ac_ref[...], pc_ref[...])

    lam = (jnp.exp(jnp.sum(lq1_ref[...] * lk1_ref[...], axis=1, keepdims=True))
           - jnp.exp(jnp.sum(lq2_ref[...] * lk2_ref[...], axis=1, keepdims=True)) + lambda_init)
    for hd in range(heads_per_step):
        inv0 = 1.0 / _all_sublanes(carry[4 * hd + 1], jnp.add)
        inv1 = 1.0 / _all_sublanes(carry[4 * hd + 3], jnp.add)
        o_t = acc_ref[2 * hd] * rows(inv0) - lam * (acc_ref[2 * hd + 1] * rows(inv1))
        o = _rms_rows(o_t.T, sg_ref[...]) * (1.0 - lambda_init)
        o_ref[:, hd * width:(hd + 1) * width] = o.astype(o_ref.dtype)


def _diff_attention(qk, vt, lq1, lk1, lq2, lk2, subln_g, *, batch, seq, heads, lambda_init, name):
    width = 2 * HEAD_DIM
    hp = min(DIFF_HEADS_PER_STEP, heads)
    groups = heads // hp
    qk3 = qk.reshape(batch, seq, 2 * heads * width)
    vt = vt.reshape(batch, seq // TK, heads * width, TK)
    vec = pl.BlockSpec((1, HEAD_DIM), lambda b, h, i: (0, 0))
    out = pl.pallas_call(
        functools.partial(_diff_attn_kernel, lambda_init=lambda_init, heads_per_step=hp,
                          lookahead=min(LOOKAHEAD, 2 * hp)),
        grid=(batch, groups, seq // TQ),
        in_specs=[vec, vec, vec, vec,
                  pl.BlockSpec((1, width), lambda b, h, i: (0, 0)),
                  pl.BlockSpec((None, TQ, hp * width), lambda b, h, i: (b, i, h)),
                  pl.BlockSpec((None, seq, hp * width), lambda b, h, i: (b, 0, groups + h)),
                  pl.BlockSpec((None, seq // TK, hp * width, TK), lambda b, h, i: (b, 0, h, 0))],
        out_specs=pl.BlockSpec((None, TQ, hp * width), lambda b, h, i: (b, i, h)),
        out_shape=jax.ShapeDtypeStruct((batch, seq, heads * width), BF16),
        scratch_shapes=[pltpu.VMEM((2 * hp, width, TQ), F32),
                        pltpu.VMEM((2 * hp, HEAD_DIM, TQ), BF16),
                        pltpu.VMEM((LOOKAHEAD, TK, TQ), F32),
                        pltpu.VMEM((TK, TQ), BF16),
                        pltpu.VMEM((SUBLANES, TQ), F32)],
        compiler_params=_params("arbitrary", "arbitrary", "arbitrary"),
        name=name,
    )(lq1.reshape(1, -1), lk1.reshape(1, -1), lq2.reshape(1, -1), lk2.reshape(1, -1),
      subln_g.reshape(1, -1), qk3, qk3, vt)
    return out.reshape(batch * seq, heads * width)


def _stick_kernel(q_ref, k_ref, vt_ref, o_ref, acc_ref, qt_ref, zc_ref, ac_ref, *, heads_per_step, lookahead):
    i = pl.program_id(2)
    tq = q_ref.shape[0]
    nv = TK // SUBLANES
    hp = heads_per_step
    last = slice((hp - 1) * HEAD_DIM, hp * HEAD_DIM)

    row = lax.broadcasted_iota(jnp.int32, (TK, tq), 0)
    col = lax.broadcasted_iota(jnp.int32, (TK, tq), 1)
    causal = ((row % SUBLANES) * nv + row // SUBLANES) < col
    sub = lax.broadcasted_iota(jnp.int32, (SUBLANES, LANES), 0)

    acc_ref[...] = jnp.zeros_like(acc_ref)
    for hh in range(hp):
        qt_ref[hh] = q_ref[:, hh * HEAD_DIM:(hh + 1) * HEAD_DIM].T

    def scores(kt, hh):
        start = pl.multiple_of(kt * TK, TK)
        k = k_ref[pl.ds(start, TK), hh * HEAD_DIM:(hh + 1) * HEAD_DIM]
        return jnp.dot(k, qt_ref[hh], preferred_element_type=F32)

    def weights(zh, right, masked):
        beta = 0.5 * jnp.tanh(zh) + 0.5
        if masked:
            beta = jnp.where(causal, beta, 0.0)
        a_cols, totals = [], []
        for ch in range(tq // LANES):
            lanes = slice(ch * LANES, (ch + 1) * LANES)
            beta3 = beta[:, lanes].reshape(nv, SUBLANES, LANES)
            run = jnp.ones((SUBLANES, LANES), F32)
            part = [None] * nv
            for v in range(nv - 1, -1, -1):
                part[v] = beta3[v] * run
                run = run - part[v]
            t = run
            for d in (1, 2, 4):
                shifted = pltpu.roll(t, SUBLANES - d, 0)
                t = t * jnp.where(sub + d < SUBLANES, shifted, 1.0)
            later = jnp.where(sub + 1 < SUBLANES, pltpu.roll(t, SUBLANES - 1, 0), 1.0)
            off = right[:, lanes] * later
            a_cols.append(jnp.concatenate([part[v] * off for v in range(nv)], axis=0))
            totals.append(jnp.broadcast_to(t[0:1, :], (SUBLANES, LANES)))
        a = jnp.concatenate(a_cols, axis=1).astype(BF16)
        return a, right * jnp.concatenate(totals, axis=1)

    def tile(kt, kt_next, right, masked, first):
        if not first:
            acc_ref[hp - 1] += jnp.dot(vt_ref[kt + 1, last, :], ac_ref[...], preferred_element_type=F32)
        new_right = []
        ahead = [zc_ref[d] for d in range(lookahead)]
        for hh in range(hp):
            z = ahead.pop(0)
            ahead.append(scores(kt, hh + lookahead) if hh + lookahead < hp
                         else scores(kt_next, hh + lookahead - hp))
            a, r = weights(z, right[hh], masked)
            new_right.append(r)
            if hh + 1 < hp:
                acc_ref[hh] += jnp.dot(vt_ref[kt, hh * HEAD_DIM:(hh + 1) * HEAD_DIM, :], a,
                                       preferred_element_type=F32)
            else:
                ac_ref[...] = a
        for d in range(lookahead):
            zc_ref[d] = ahead[d]
        return tuple(new_right)

    for d in range(lookahead):
        zc_ref[d] = scores(i, d)
    ones = tuple(jnp.ones((SUBLANES, tq), F32) for _ in range(hp))
    right = tile(i, jnp.maximum(i - 1, 0), ones, True, True)
    lax.fori_loop(0, i, lambda s, rt: tile(i - 1 - s, jnp.maximum(i - 2 - s, 0), rt, False, False), right)
    acc_ref[hp - 1] += jnp.dot(vt_ref[0, last, :], ac_ref[...], preferred_element_type=F32)
    for hh in range(hp):
        o_ref[:, hh * HEAD_DIM:(hh + 1) * HEAD_DIM] = acc_ref[hh].T.astype(o_ref.dtype)


def _stick_breaking(q, kp, vt, *, batch, seq, heads, name):
    width = heads * HEAD_DIM
    hp = min(STICK_HEADS_PER_STEP, heads)
    q3 = q.reshape(batch, seq, width)
    kp = kp.reshape(batch, seq, width)
    vt = vt.reshape(batch, seq // TK, width, TK)
    out = pl.pallas_call(
        functools.partial(_stick_kernel, heads_per_step=hp, lookahead=min(LOOKAHEAD, hp)),
        grid=(batch, heads // hp, seq // TQ),
        in_specs=[pl.BlockSpec((None, TQ, hp * HEAD_DIM), lambda b, h, i: (b, i, h)),
                  pl.BlockSpec((None, seq, hp * HEAD_DIM), lambda b, h, i: (b, 0, h)),
                  pl.BlockSpec((None, seq // TK, hp * HEAD_DIM, TK), lambda b, h, i: (b, 0, h, 0))],
        out_specs=pl.BlockSpec((None, TQ, hp * HEAD_DIM), lambda b, h, i: (b, i, h)),
        out_shape=jax.ShapeDtypeStruct((batch, seq, width), BF16),
        scratch_shapes=[pltpu.VMEM((hp, HEAD_DIM, TQ), F32),
                        pltpu.VMEM((hp, HEAD_DIM, TQ), BF16),
                        pltpu.VMEM((LOOKAHEAD, TK, TQ), F32),
                        pltpu.VMEM((TK, TQ), BF16)],
        compiler_params=_params("arbitrary", "arbitrary", "arbitrary"),
        name=name,
    )(q3, kp, vt)
    return out.reshape(batch * seq, width)


def _kv_kernel(x_ref, g_ref, wk_ref, wvt_ref, k_ref, vt_ref, xn_ref, *, n_k_blocks):
    j = pl.program_id(1)
    bm = x_ref.shape[0]
    nv = TK // SUBLANES

    @pl.when(j == 0)
    def _():
        xn = _rms_rows(x_ref[...], g_ref[...]).astype(BF16)
        r = lax.broadcasted_iota(jnp.int32, (TK, TK), 0)
        t = lax.broadcasted_iota(jnp.int32, (TK, TK), 1)
        pick = jnp.where(t == (r % SUBLANES) * nv + r // SUBLANES, 1.0, 0.0).astype(BF16)
        for grp in range(bm // TK):
            rows = slice(grp * TK, (grp + 1) * TK)
            xn_ref[rows, :] = jnp.dot(pick, xn[rows, :], preferred_element_type=F32).astype(BF16)

    @pl.when(j < n_k_blocks)
    def _():
        k_ref[...] = jnp.dot(xn_ref[...], wk_ref[...], preferred_element_type=F32).astype(k_ref.dtype)

    @pl.when(j >= n_k_blocks)
    def _():
        acc = lax.dot_general(wvt_ref[...], xn_ref[...], (((1,), (1,)), ((), ())),
                              preferred_element_type=F32)
        for t in range(vt_ref.shape[0]):
            vt_ref[t] = acc[:, t * TK:(t + 1) * TK].astype(vt_ref.dtype)


def _kv_proj(x, gain, wk, wvt, *, name):
    tokens, d = x.shape
    n = wk.shape[1]
    bm, bn = min(BM, tokens), min(BN, n)
    assert tokens % bm == 0 and n % bn == 0 and bm % TK == 0, (tokens, bm, n, bn)
    nk = n // bn
    return pl.pallas_call(
        functools.partial(_kv_kernel, n_k_blocks=nk),
        grid=(tokens // bm, 2 * nk),
        in_specs=[pl.BlockSpec((bm, d), lambda i, j: (i, 0)),
                  pl.BlockSpec((1, d), lambda i, j: (0, 0)),
                  pl.BlockSpec((d, bn), lambda i, j: (0, jnp.minimum(j, nk - 1))),
                  pl.BlockSpec((bn, d), lambda i, j: (jnp.maximum(j - nk, 0), 0))],
        out_specs=[pl.BlockSpec((bm, bn), lambda i, j: (i, jnp.minimum(j, nk - 1))),
                   pl.BlockSpec((bm // TK, bn, TK), lambda i, j: (i, jnp.maximum(j - nk, 0), 0))],
        out_shape=[jax.ShapeDtypeStruct((tokens, n), BF16),
                   jax.ShapeDtypeStruct((tokens // TK, n, TK), BF16)],
        scratch_shapes=[pltpu.VMEM((bm, d), BF16)],
        compiler_params=_params("arbitrary", "arbitrary"),
        name=name,
    )(x, gain.reshape(1, d), wk, wvt)


def kernel(x, positions, attn_norm_g, ffn_norm_g, a_w_qkv, a_q_norm_g, a_k_norm_g, a_lambda_q1, a_lambda_k1, a_lambda_q2, a_lambda_k2, a_subln_g, a_w_o, kv_norm_g, b_w_kv, b_w_q, b_w_o, ffn_w_up, ffn_conv_w, ffn_conv_b, ffn_w_down):
    batch, seq, d_model = x.shape
    depth = attn_norm_g.shape[0]
    n_a = a_w_qkv.shape[0]
    a_heads = d_model // (2 * HEAD_DIM)
    b_heads = d_model // HEAD_DIM
    qk_width = a_heads * 2 * HEAD_DIM
    scale = HEAD_DIM ** -0.5

    h = x.reshape(batch * seq, d_model)
    tables = _rotary_tables(positions)
    order = _head_dim_order()
    kp = vt = None
    for layer in range(depth):
        if layer < n_a:
            lambda_init = 0.8 - 0.6 * math.exp(-0.3 * layer)
            n_qk_heads = 2 * qk_width // HEAD_DIM
            col_order = (jnp.arange(n_qk_heads)[:, None] * HEAD_DIM + order[None, :]).reshape(-1)
            head_gain = jnp.concatenate([
                jnp.tile(a_q_norm_g[layer][order] * (scale * math.log2(math.e)), qk_width // HEAD_DIM),
                jnp.tile(a_k_norm_g[layer][order], qk_width // HEAD_DIM)]).reshape(1, -1)
            w_qkv = a_w_qkv[layer]
            qk, vt_a = _qkv_proj(h, attn_norm_g[layer], w_qkv[:, col_order].astype(BF16),
                                 w_qkv[:, 2 * qk_width:].T.astype(BF16), head_gain, tables,
                                 name=f"a{layer}_qkv")
            o = _diff_attention(qk, vt_a, a_lambda_q1[layer], a_lambda_k1[layer], a_lambda_q2[layer],
                                a_lambda_k2[layer], a_subln_g[layer], batch=batch, seq=seq,
                                heads=a_heads, lambda_init=lambda_init, name=f"a{layer}_attn")
            h = _proj_residual(o, a_w_o[layer].astype(BF16), h, bm=BM_SQUARE, bn=BN_SQUARE,
                               name=f"a{layer}_out")
        else:
            if layer == n_a:
                b_width = b_heads * HEAD_DIM
                kp, vt = _kv_proj(h, kv_norm_g, b_w_kv[:, :b_width].astype(BF16),
                                  b_w_kv[:, b_width:].T.astype(BF16), name="b_kv")
            j = layer - n_a
            q = _norm_proj(h, attn_norm_g[layer], b_w_q[j].astype(BF16), out_scale=0.5 * scale,
                           bm=BM_SQUARE, bn=BN_SQUARE, name=f"b{j}_q")
            o = _stick_breaking(q, kp, vt, batch=batch, seq=seq, heads=b_heads, name=f"b{j}_attn")
            h = _proj_residual(o, b_w_o[j].astype(BF16), h, bm=BM_SQUARE, bn=BN_SQUARE, name=f"b{j}_out")
        act = _ffn_up(h, ffn_norm_g[layer], ffn_w_up[layer].astype(BF16), ffn_conv_w[layer],
                      ffn_conv_b[layer], seq=seq, name=f"l{layer}_ffn_up")
        h = _proj_residual(act, ffn_w_down[layer].astype(BF16), h, bn=BN_DOWN, name=f"l{layer}_ffn_down")
    return h.reshape(batch, seq, d_model)
```
